```python
import math
import jax, jax.numpy as jnp
from jax import lax
import numpy as np

D_MODEL = 4096
BATCH = 8
SEQ = 2048
DEPTH = 2

N_META = 16
SSD_HEADDIM = 64
SSD_INNER = D_MODEL
SSD_HEADS = SSD_INNER // SSD_HEADDIM
SSD_GROUPS = 8
SSD_HPG = SSD_HEADS // SSD_GROUPS
SSD_STATE = 128
SSD_CONV = 4
SSD_CHUNK = 128
SSD_CONV_CH = SSD_INNER + 2 * SSD_GROUPS * SSD_STATE
FOX_HEADDIM = 128
FOX_WIDTH = D_MODEL // 2
FOX_HEADS = FOX_WIDTH // FOX_HEADDIM
ATT_BLOCK = 128
NEG_INF = -1e30
GLA_HEADS = D_MODEL // 512
GLA_DK = 256
GLA_DV = 512
GLA_KWIDTH = GLA_HEADS * GLA_DK
GLA_VWIDTH = GLA_HEADS * GLA_DV
GLA_RANK = 16
GLA_TAU = 16.0
GLA_CHUNK = 64
CONV_CH = D_MODEL // 2
CONV_WIDTH = 31
D_FF = -(-8 * D_MODEL // (3 * 256)) * 256
EV_SPLITS = (SSD_INNER, SSD_CONV_CH, SSD_HEADS, FOX_WIDTH, FOX_WIDTH, FOX_WIDTH, FOX_HEADS)
EV_PROJ = SSD_INNER + SSD_CONV_CH + SSD_HEADS + 3 * FOX_WIDTH + FOX_HEADS
EV_MIX = SSD_INNER + FOX_WIDTH
OD_SPLITS = (GLA_KWIDTH, GLA_KWIDTH, GLA_VWIDTH, GLA_RANK, GLA_VWIDTH, 2 * CONV_CH)
OD_PROJ = 2 * GLA_KWIDTH + 2 * GLA_VWIDTH + GLA_RANK + 2 * CONV_CH
OD_MIX = GLA_VWIDTH + CONV_CH
LN_EPS = 1e-5

kernel_name = 'hybrid_ssd_fox_gla_conformer_deepnorm'


def split_last(t, sizes):
    return jnp.split(t, np.cumsum(sizes)[:-1].tolist(), axis=-1)


def pad_front(t, n):
    return jnp.pad(t, [(0, 0), (n, 0)] + [(0, 0)] * (t.ndim - 2))


def to_chunks(t, q):
    b, lp = t.shape[:2]
    return jnp.moveaxis(t.reshape((b, lp // q, q) + t.shape[2:]), 1, 0)


def from_chunks(t):
    c, b, q = t.shape[:3]
    return jnp.moveaxis(t, 0, 1).reshape((b, c * q) + t.shape[3:])


def layer_norm(x, g, b):
    xf = x.astype(jnp.float32)
    xc = xf - jnp.mean(xf, -1, keepdims=True)
    var = jnp.mean(xc * xc, -1, keepdims=True)
    return (xc * lax.rsqrt(var + LN_EPS) * g.astype(jnp.float32) + b.astype(jnp.float32)).astype(x.dtype)


def rms_norm_groups(x, w, groups):
    shp = x.shape
    xf = x.astype(jnp.float32).reshape(shp[:-1] + (groups, shp[-1] // groups))
    xf = xf * lax.rsqrt(jnp.mean(xf * xf, -1, keepdims=True) + LN_EPS)
    return xf.reshape(shp) * w.astype(jnp.float32)


def causal_depthwise_conv(u, w, bias):
    width, ch = w.shape
    up = jnp.pad(u, ((0, 0), (width - 1, 0), (0, 0)))
    out = lax.conv_general_dilated(up, w[:, None, :].astype(u.dtype), window_strides=(1,), padding='VALID',
                                   dimension_numbers=('NWC', 'WIO', 'NWC'), feature_group_count=ch)
    return out + bias


def ssd_chunked(x, log_a, bm, cm):
    b, lp, g, r, p = x.shape
    n = bm.shape[-1]
    causal = jnp.tril(jnp.ones((SSD_CHUNK, SSD_CHUNK), dtype=bool))[None, :, :, None, None]

    def step(state, inp):
        xc, ac, bc, cc = inp
        a_cs = jnp.cumsum(ac, axis=1)
        seg = jnp.where(causal, a_cs[:, :, None] - a_cs[:, None, :], -jnp.inf)
        cb = jnp.einsum('bign,bjgn->bijg', cc, bc)
        y = (jnp.einsum('bijg,bijgr,bjgrp->bigrp', cb, jnp.exp(seg), xc)
             + jnp.einsum('bign,bgrpn,bigr->bigrp', cc, state, jnp.exp(a_cs)))
        a_end = a_cs[:, -1]
        state = (state * jnp.exp(a_end)[..., None, None]
                 + jnp.einsum('bjgn,bjgr,bjgrp->bgrpn', bc, jnp.exp(a_end[:, None] - a_cs), xc))
        return state, y

    s0 = jnp.zeros((b, g, r, p, n), jnp.float32)
    _, y = lax.scan(step, s0, (to_chunks(x, SSD_CHUNK), to_chunks(log_a, SSD_CHUNK),
                               to_chunks(bm, SSD_CHUNK), to_chunks(cm, SSD_CHUNK)))
    return from_chunks(y)


def gla_chunked(q, k, v, g):
    b, lp, nh, dk = q.shape
    dv = v.shape[-1]
    causal = jnp.tril(jnp.ones((GLA_CHUNK, GLA_CHUNK), dtype=bool))[None, None]
    mid = GLA_CHUNK // 2

    def step(state, inp):
        qc, kc, vc, gc = inp
        bcum = jnp.cumsum(gc, axis=1)
        bref = bcum[:, mid:mid + 1]
        qe = qc * jnp.exp(bcum - bref)
        ke = kc * jnp.exp(bref - bcum)
        scores = jnp.where(causal, jnp.einsum('bihd,bjhd->bhij', qe, ke), 0.0)
        o = (jnp.einsum('bhij,bjhv->bihv', scores, vc)
             + jnp.einsum('bihd,bhdv->bihv', qc * jnp.exp(bcum), state))
        btot = bcum[:, -1]
        state = (state * jnp.exp(btot)[..., None]
                 + jnp.einsum('bjhd,bjhv->bhdv', kc * jnp.exp(btot[:, None] - bcum), vc))
        return state, o

    s0 = jnp.zeros((b, nh, dk, dv), jnp.float32)
    _, o = lax.scan(step, s0, (to_chunks(q, GLA_CHUNK), to_chunks(k, GLA_CHUNK),
                               to_chunks(v, GLA_CHUNK), to_chunks(g, GLA_CHUNK)))
    return from_chunks(o)


def fox_attention(q, k, v, log_f, n_pad):
    b, lp, nh, dh = q.shape
    cum = jnp.cumsum(log_f, axis=1).transpose(0, 2, 1)
    kpos = jnp.arange(lp)
    scale = dh ** -0.5

    def one_block(i):
        start = i * ATT_BLOCK
        qb = lax.dynamic_slice_in_dim(q, start, ATT_BLOCK, axis=1)
        cq = lax.dynamic_slice_in_dim(cum, start, ATT_BLOCK, axis=2)
        qpos = start + jnp.arange(ATT_BLOCK)
        s = (jnp.einsum('bqhd,bkhd->bhqk', qb, k).astype(jnp.float32) * scale
             + cq[..., :, None] - cum[:, :, None, :])
        mask = (kpos[None, :] <= qpos[:, None]) & (kpos[None, :] >= n_pad)
        p = jax.nn.softmax(jnp.where(mask, s, NEG_INF), axis=-1)
        return jnp.einsum('bhqk,bkhd->bqhd', p.astype(v.dtype), v)

    out = lax.map(one_block, jnp.arange(lp // ATT_BLOCK))
    return from_chunks(out)


def ssd_fox_mixer(h, w_in, conv_w, conv_b, dt_bias, a_log, d_skip, norm_w, fgate_b, w_out):
    b, L, _ = h.shape
    z, xbc, dt_raw, q, k, v, f_raw = split_last(h @ w_in, EV_SPLITS)
    xbc = jax.nn.silu(causal_depthwise_conv(xbc, conv_w, conv_b))
    xs, bm, cm = split_last(xbc, (SSD_INNER, SSD_GROUPS * SSD_STATE, SSD_GROUPS * SSD_STATE))
    dt = jax.nn.softplus((dt_raw + dt_bias).astype(jnp.float32)).reshape(b, L, SSD_GROUPS, SSD_HPG)
    log_a = dt * (-jnp.exp(a_log.astype(jnp.float32))).reshape(SSD_GROUPS, SSD_HPG)
    xh = xs.reshape(b, L, SSD_GROUPS, SSD_HPG, SSD_HEADDIM)
    pad = SSD_CHUNK - N_META
    y = ssd_chunked(pad_front(xh * dt[..., None], pad), pad_front(log_a, pad),
                    pad_front(bm.reshape(b, L, SSD_GROUPS, SSD_STATE), pad),
                    pad_front(cm.reshape(b, L, SSD_GROUPS, SSD_STATE), pad))[:, pad:]
    y = y + xh * d_skip.reshape(SSD_GROUPS, SSD_HPG)[..., None]
    y = rms_norm_groups(y.reshape(b, L, SSD_INNER) * jax.nn.silu(z.astype(jnp.float32)), norm_w, SSD_GROUPS)
    log_f = jax.nn.log_sigmoid((f_raw + fgate_b).astype(jnp.float32))
    pad = ATT_BLOCK - N_META

    def heads(t):
        return pad_front(t.reshape(b, L, FOX_HEADS, FOX_HEADDIM), pad)

    o = fox_attention(heads(q), heads(k), heads(v), pad_front(log_f, pad), pad)[:, pad:]
    mixed = jnp.concatenate([y.astype(h.dtype), o.reshape(b, L, FOX_WIDTH).astype(h.dtype)], axis=-1)
    return mixed @ w_out


def gla_conv_mixer(h, w_in, w_gate_up, gate_b, gla_norm_w, dw_w, dw_b, conv_ln_g, conv_ln_b, w_out):
    b, L, _ = h.shape
    q, k, v, g_lr, r, glu = split_last(h @ w_in, OD_SPLITS)
    g = jax.nn.log_sigmoid((g_lr @ w_gate_up + gate_b).astype(jnp.float32)) / GLA_TAU
    pad = GLA_CHUNK - N_META

    def heads(t, d):
        return pad_front(t.reshape(b, L, GLA_HEADS, d), pad)

    o = gla_chunked(heads(q * GLA_DK ** -0.5, GLA_DK), heads(k, GLA_DK),
                    heads(v, GLA_DV), heads(g, GLA_DK))[:, pad:]
    o = rms_norm_groups(o, gla_norm_w, 1).reshape(b, L, GLA_VWIDTH) * jax.nn.silu(r.astype(jnp.float32))
    u_a, u_g = split_last(glu, (CONV_CH, CONV_CH))
    u = causal_depthwise_conv(u_a * jax.nn.sigmoid(u_g), dw_w, dw_b)
    u = jax.nn.silu(layer_norm(u, conv_ln_g, conv_ln_b))
    mixed = jnp.concatenate([o.astype(h.dtype), u.astype(h.dtype)], axis=-1)
    return mixed @ w_out


def swiglu(h, w_gate, w_up, w_down):
    return (jax.nn.silu(h @ w_gate) * (h @ w_up)) @ w_down


def setup_inputs(seed: int = 0) -> dict:
    key = jax.random.key(seed)
    ks = jax.random.split(key, 32)
    kit = iter(range(32))
    ne, no = (DEPTH + 1) // 2, DEPTH // 2
    beta = (8.0 * DEPTH) ** -0.25

    def nrm(shape, scale):
        return jax.random.normal(ks[next(kit)], shape, jnp.float32) * scale

    x = nrm((BATCH, SEQ, D_MODEL), 1.0)
    meta_tokens = nrm((N_META, D_MODEL), 1.0)
    ev_w_in = nrm((ne, D_MODEL, EV_PROJ), D_MODEL ** -0.5)
    ev_conv_w = nrm((ne, SSD_CONV, SSD_CONV_CH), SSD_CONV ** -0.5)
    ev_conv_b = nrm((ne, SSD_CONV_CH), 0.02)
    dt0 = jnp.exp(jax.random.uniform(ks[next(kit)], (ne, SSD_HEADS), jnp.float32,
                                     minval=math.log(1e-3), maxval=math.log(1e-1)))
    ev_dt_bias = dt0 + jnp.log(-jnp.expm1(-dt0))
    ev_a_log = jnp.log(jax.random.uniform(ks[next(kit)], (ne, SSD_HEADS), jnp.float32, minval=1.0, maxval=16.0))
    ev_d_skip = 1.0 + nrm((ne, SSD_HEADS), 0.02)
    ev_ssm_norm_w = 1.0 + nrm((ne, SSD_INNER), 0.02)
    ev_fgate_b = nrm((ne, FOX_HEADS), 0.1)
    ev_w_out = nrm((ne, EV_MIX, D_MODEL), EV_MIX ** -0.5 * beta)
    od_w_in = nrm((no, D_MODEL, OD_PROJ), D_MODEL ** -0.5)
    od_w_gate_up = nrm((no, GLA_RANK, GLA_KWIDTH), GLA_RANK ** -0.5)
    od_gate_b = nrm((no, GLA_KWIDTH), 0.1)
    od_gla_norm_w = 1.0 + nrm((no, GLA_DV), 0.02)
    od_dwconv_w = nrm((no, CONV_WIDTH, CONV_CH), CONV_WIDTH ** -0.5)
    od_dwconv_b = nrm((no, CONV_CH), 0.02)
    od_conv_ln_g = 1.0 + nrm((no, CONV_CH), 0.02)
    od_conv_ln_b = nrm((no, CONV_CH), 0.02)
    od_w_out = nrm((no, OD_MIX, D_MODEL), OD_MIX ** -0.5 * beta)
    ln_mix_g = 1.0 + nrm((DEPTH, D_MODEL), 0.02)
    ln_mix_b = nrm((DEPTH, D_MODEL), 0.02)
    ffn_w_gate = nrm((DEPTH, D_MODEL, D_FF), D_MODEL ** -0.5)
    ffn_w_up = nrm((DEPTH, D_MODEL, D_FF), D_MODEL ** -0.5)
    ffn_w_down = nrm((DEPTH, D_FF, D_MODEL), D_FF ** -0.5 * beta)
    ln_ffn_g = 1.0 + nrm((DEPTH, D_MODEL), 0.02)
    ln_ffn_b = nrm((DEPTH, D_MODEL), 0.02)
    return {'x': x, 'meta_tokens': meta_tokens,
            'ev_w_in': ev_w_in, 'ev_conv_w': ev_conv_w, 'ev_conv_b': ev_conv_b, 'ev_dt_bias': ev_dt_bias,
            'ev_a_log': ev_a_log, 'ev_d_skip': ev_d_skip, 'ev_ssm_norm_w': ev_ssm_norm_w,
            'ev_fgate_b': ev_fgate_b, 'ev_w_out': ev_w_out,
            'od_w_in': od_w_in, 'od_w_gate_up': od_w_gate_up, 'od_gate_b': od_gate_b,
            'od_gla_norm_w': od_gla_norm_w, 'od_dwconv_w': od_dwconv_w, 'od_dwconv_b': od_dwconv_b,
            'od_conv_ln_g': od_conv_ln_g, 'od_conv_ln_b': od_conv_ln_b, 'od_w_out': od_w_out,
            'ln_mix_g': ln_mix_g, 'ln_mix_b': ln_mix_b, 'ffn_w_gate': ffn_w_gate, 'ffn_w_up': ffn_w_up,
            'ffn_w_down': ffn_w_down, 'ln_ffn_g': ln_ffn_g, 'ln_ffn_b': ln_ffn_b}


def reference(x, meta_tokens, ev_w_in, ev_conv_w, ev_conv_b, ev_dt_bias, ev_a_log, ev_d_skip, ev_ssm_norm_w,
              ev_fgate_b, ev_w_out, od_w_in, od_w_gate_up, od_gate_b, od_gla_norm_w, od_dwconv_w, od_dwconv_b,
              od_conv_ln_g, od_conv_ln_b, od_w_out, ln_mix_g, ln_mix_b, ffn_w_gate, ffn_w_up, ffn_w_down,
              ln_ffn_g, ln_ffn_b):
    alpha = (2.0 * DEPTH) ** 0.25
    b = x.shape[0]
    meta = jnp.broadcast_to(meta_tokens[None].astype(x.dtype), (b, N_META, x.shape[-1]))
    h = jnp.concatenate([meta, x], axis=1)
    for i in range(DEPTH):
        j = i // 2
        if i % 2 == 0:
            mix = ssd_fox_mixer(h, ev_w_in[j], ev_conv_w[j], ev_conv_b[j], ev_dt_bias[j], ev_a_log[j],
                                ev_d_skip[j], ev_ssm_norm_w[j], ev_fgate_b[j], ev_w_out[j])
        else:
            mix = gla_conv_mixer(h, od_w_in[j], od_w_gate_up[j], od_gate_b[j], od_gla_norm_w[j],
                                 od_dwconv_w[j], od_dwconv_b[j], od_conv_ln_g[j], od_conv_ln_b[j], od_w_out[j])
        h = layer_norm(alpha * h + mix, ln_mix_g[i], ln_mix_b[i])
        h = layer_norm(alpha * h + swiglu(h, ffn_w_gate[i], ffn_w_up[i], ffn_w_down[i]), ln_ffn_g[i], ln_ffn_b[i])
    return h[:, N_META:]
```

```python
import functools

import jax
import jax.numpy as jnp
from jax import lax
from jax.experimental import pallas as pl
from jax.experimental.pallas import tpu as pltpu

F32 = jnp.float32
BF16 = jnp.bfloat16
HIGHEST = lax.Precision.HIGHEST

D_MODEL = 4096
DEPTH = 2
N_META = 16
SSD_HEADDIM = 64
SSD_INNER = D_MODEL
SSD_HEADS = SSD_INNER // SSD_HEADDIM
SSD_GROUPS = 8
SSD_HPG = SSD_HEADS // SSD_GROUPS
SSD_STATE = 128
SSD_CONV = 4
SSD_GROUP_W = SSD_HPG * SSD_HEADDIM
SSD_CONV_CH = SSD_INNER + 2 * SSD_GROUPS * SSD_STATE
FOX_HEADDIM = 128
FOX_WIDTH = D_MODEL // 2
FOX_HEADS = FOX_WIDTH // FOX_HEADDIM
NEG_INF = -1e30
GLA_HEADS = D_MODEL // 512
GLA_DK = 256
GLA_DV = 512
GLA_KWIDTH = GLA_HEADS * GLA_DK
GLA_VWIDTH = GLA_HEADS * GLA_DV
GLA_RANK = 16
GLA_TAU = 16.0
GLA_CHUNK = 64
CONV_CH = D_MODEL // 2
CONV_WIDTH = 31
D_FF = -(-8 * D_MODEL // (3 * 256)) * 256
LN_EPS = 1e-5

LANES = 128
TBLK = 128
CONV_HALO = 32
MIB = 1024 * 1024


def _cparams(semantics, vmem_mib):
    return pltpu.CompilerParams(dimension_semantics=semantics, vmem_limit_bytes=vmem_mib * MIB)


def _tile(dim, target):
    return dim if dim < target else target


def _sigmoid(x):
    return 1.0 / (1.0 + jnp.exp(-x))


def _silu(x):
    return x * _sigmoid(x)


def _softplus(x):
    return jnp.maximum(x, 0.0) + jnp.log1p(jnp.exp(-jnp.abs(x)))


def _log_sigmoid(x):
    return jnp.minimum(x, 0.0) - jnp.log1p(jnp.exp(-jnp.abs(x)))


def _split_dot(v, onehot, parts):
    out = None
    rem = v
    for _ in range(parts):
        piece = rem.astype(BF16)
        term = jnp.dot(piece, onehot, preferred_element_type=F32)
        out = term if out is None else out + term
        rem = rem - piece.astype(F32)
    return out


def _tril(n):
    r = lax.broadcasted_iota(jnp.int32, (n, n), 0)
    c = lax.broadcasted_iota(jnp.int32, (n, n), 1)
    return c <= r


def _mm_kernel(x_ref, w_ref, o_ref):
    o_ref[...] = jnp.dot(x_ref[...], w_ref[...], preferred_element_type=F32).astype(o_ref.dtype)


def _matmul(x, w, *, bm, bn, out_dtype, vmem_mib, name):
    m, k = x.shape
    n = w.shape[1]
    bm, bn = _tile(m, bm), _tile(n, bn)
    return pl.pallas_call(
        _mm_kernel,
        grid=(pl.cdiv(m, bm), pl.cdiv(n, bn)),
        in_specs=[pl.BlockSpec((bm, k), lambda i, j: (i, 0)),
                  pl.BlockSpec((k, bn), lambda i, j: (0, j))],
        out_specs=pl.BlockSpec((bm, bn), lambda i, j: (i, j)),
        out_shape=jax.ShapeDtypeStruct((m, n), out_dtype),
        compiler_params=_cparams(("parallel", "arbitrary"), vmem_mib),
        name=name,
    )(x, w)


def _mm2_kernel(a_ref, c_ref, w1_ref, w2_ref, o_ref):
    acc = jnp.dot(a_ref[...], w1_ref[...], preferred_element_type=F32)
    acc = acc + jnp.dot(c_ref[...], w2_ref[...], preferred_element_type=F32)
    o_ref[...] = acc.astype(o_ref.dtype)


def _matmul2(a, c, w1, w2, *, bm, bn, vmem_mib, name):
    m, k1 = a.shape
    k2 = c.shape[1]
    n = w1.shape[1]
    bm, bn = _tile(m, bm), _tile(n, bn)
    return pl.pallas_call(
        _mm2_kernel,
        grid=(pl.cdiv(m, bm), pl.cdiv(n, bn)),
        in_specs=[pl.BlockSpec((bm, k1), lambda i, j: (i, 0)),
                  pl.BlockSpec((bm, k2), lambda i, j: (i, 0)),
                  pl.BlockSpec((k1, bn), lambda i, j: (0, j)),
                  pl.BlockSpec((k2, bn), lambda i, j: (0, j))],
        out_specs=pl.BlockSpec((bm, bn), lambda i, j: (i, j)),
        out_shape=jax.ShapeDtypeStruct((m, n), F32),
        compiler_params=_cparams(("parallel", "arbitrary"), vmem_mib),
        name=name,
    )(a, c, w1, w2)


def _swiglu_up_kernel(x_ref, wg_ref, wu_ref, o_ref):
    x = x_ref[...]
    g = jnp.dot(x, wg_ref[...], preferred_element_type=F32)
    u = jnp.dot(x, wu_ref[...], preferred_element_type=F32)
    o_ref[...] = (_silu(g) * u).astype(o_ref.dtype)


def _swiglu_up(x, wg, wu, *, bm, bn, vmem_mib, name):
    m, k = x.shape
    n = wg.shape[1]
    bm, bn = _tile(m, bm), _tile(n, bn)
    return pl.pallas_call(
        _swiglu_up_kernel,
        grid=(pl.cdiv(m, bm), pl.cdiv(n, bn)),
        in_specs=[pl.BlockSpec((bm, k), lambda i, j: (i, 0)),
                  pl.BlockSpec((k, bn), lambda i, j: (0, j)),
                  pl.BlockSpec((k, bn), lambda i, j: (0, j))],
        out_specs=pl.BlockSpec((bm, bn), lambda i, j: (i, j)),
        out_shape=jax.ShapeDtypeStruct((m, n), BF16),
        compiler_params=_cparams(("parallel", "arbitrary"), vmem_mib),
        name=name,
    )(x, wg, wu)


def _add_ln_kernel(h_ref, m_ref, g_ref, b_ref, of_ref, ob_ref, *, alpha):
    y = alpha * h_ref[...] + m_ref[...]
    yc = y - jnp.mean(y, axis=-1, keepdims=True)
    var = jnp.mean(yc * yc, axis=-1, keepdims=True)
    out = yc * lax.rsqrt(var + LN_EPS) * g_ref[...] + b_ref[...]
    of_ref[...] = out
    ob_ref[...] = out.astype(BF16)


def _add_ln(h, mix, g, b, *, alpha, name):
    m, d = h.shape
    bm = _tile(m, 128)
    row = pl.BlockSpec((bm, d), lambda i: (i, 0))
    vec = pl.BlockSpec((1, d), lambda i: (0, 0))
    return pl.pallas_call(
        functools.partial(_add_ln_kernel, alpha=alpha),
        grid=(pl.cdiv(m, bm),),
        in_specs=[row, row, vec, vec],
        out_specs=[row, row],
        out_shape=[jax.ShapeDtypeStruct((m, d), F32), jax.ShapeDtypeStruct((m, d), BF16)],
        compiler_params=_cparams(("parallel",), 32),
        name=name,
    )(h, mix, g.reshape(1, d), b.reshape(1, d))


def _ssd_kernel(xs_ref, bm_ref, cm_ref, z_ref, dtf_ref, sel_ref, e_ref,
                cwx_ref, cwb_ref, cwc_ref, cbx_ref, cbb_ref, cbc_ref,
                dtb_ref, alog_ref, dexp_ref, nw_ref, y_ref,
                s_ref, ex_ref, eb_ref, ec_ref, *, seq_len):
    c = pl.program_id(2)

    @pl.when(c == 0)
    def _():
        s_ref[...] = jnp.zeros_like(s_ref)
        ex_ref[0:8, :] = jnp.zeros((8, SSD_GROUP_W), F32)
        eb_ref[0:8, :] = jnp.zeros((8, SSD_STATE), F32)
        ec_ref[0:8, :] = jnp.zeros((8, SSD_STATE), F32)

    valid = seq_len - c * TBLK
    rmask = lax.broadcasted_iota(jnp.int32, (TBLK, 1), 0) < valid

    def conv_silu(raw_ref, ext_ref, w_ref, bias_ref):
        raw = jnp.where(rmask, raw_ref[0], 0.0)
        ext_ref[8:8 + TBLK, :] = raw
        acc = bias_ref[...] + w_ref[SSD_CONV - 1:SSD_CONV, :] * raw
        for k in range(SSD_CONV - 1):
            lo = 8 - (SSD_CONV - 1) + k
            acc = acc + w_ref[k:k + 1, :] * ext_ref[lo:lo + TBLK, :]
        ext_ref[0:8, :] = raw[TBLK - 8:TBLK, :]
        return _silu(acc)

    xs = conv_silu(xs_ref, ex_ref, cwx_ref, cbx_ref)
    bmat = conv_silu(bm_ref, eb_ref, cwb_ref, cbb_ref)
    cmat = conv_silu(cm_ref, ec_ref, cwc_ref, cbc_ref)

    dt_all = jnp.where(rmask, _softplus(dtf_ref[0] + dtb_ref[...]), 0.0)
    la_all = dt_all * (-jnp.exp(alog_ref[...]))
    sel = sel_ref[0]
    dt_g = _split_dot(dt_all, sel, 3)
    la_g = _split_dot(la_all, sel, 3)
    tri = _tril(TBLK)
    a_cs = jnp.dot(tri.astype(F32), la_g, precision=HIGHEST, preferred_element_type=F32)
    a_cs_t = a_cs.T
    e_in = jnp.exp(a_cs)
    e_out = jnp.exp(a_cs[TBLK - 1:TBLK, :] - a_cs)
    expanded = _split_dot(jnp.concatenate([dt_g, e_in, e_out], axis=0), e_ref[...], 2)
    dt_e = expanded[0:TBLK]
    ein_e = expanded[TBLK:2 * TBLK]
    eout_e = expanded[2 * TBLK:3 * TBLK]

    xdt = xs * dt_e
    bb = bmat.astype(BF16)
    cc = cmat.astype(BF16)
    state = s_ref[...]
    y = jnp.dot(cc, state.astype(BF16), preferred_element_type=F32) * ein_e
    cb = lax.dot_general(cc, bb, (((1,), (1,)), ((), ())), preferred_element_type=F32)
    lane = lax.broadcasted_iota(jnp.int32, (TBLK, LANES), 1)
    cols = []
    for m in range(SSD_GROUP_W // LANES):
        xcol = xdt[:, m * LANES:(m + 1) * LANES]
        ycol = None
        for half in range(LANES // SSD_HEADDIM):
            r = m * (LANES // SSD_HEADDIM) + half
            colb = jnp.broadcast_to(a_cs[:, r:r + 1], (TBLK, TBLK))
            rowb = jnp.broadcast_to(a_cs_t[r:r + 1, :], (TBLK, TBLK))
            lmat = jnp.where(tri, jnp.exp(colb - rowb), 0.0) * cb
            in_half = (lane >= half * SSD_HEADDIM) & (lane < (half + 1) * SSD_HEADDIM)
            xh = jnp.where(in_half, xcol, 0.0).astype(BF16)
            term = jnp.dot(lmat.astype(BF16), xh, preferred_element_type=F32)
            ycol = term if ycol is None else ycol + term
        cols.append(ycol)
    y = y + jnp.concatenate(cols, axis=1) + xs * dexp_ref[...]

    xd = (xdt * eout_e).astype(BF16)
    s_ref[...] = state * ein_e[TBLK - 1:TBLK, :] + lax.dot_general(
        bb, xd, (((0,), (0,)), ((), ())), preferred_element_type=F32)

    yz = y * _silu(z_ref[0])
    ms = jnp.mean(yz * yz, axis=-1, keepdims=True)
    y_ref[0] = (yz * lax.rsqrt(ms + LN_EPS) * nw_ref[...]).astype(y_ref.dtype)


def _ssd(p_main, p_small, conv_w, conv_b, dt_bias, a_log, d_skip, norm_w, *, xbc_col, z_col):
    bsz, seq_len, _ = p_main.shape
    nc = pl.cdiv(seq_len, TBLK)
    gw, st = SSD_GROUP_W, SSD_STATE
    xoff, boff, coff = xbc_col // gw, (xbc_col + SSD_INNER) // st, (xbc_col + SSD_INNER + SSD_GROUPS * st) // st
    zoff = z_col // gw
    head = jnp.arange(LANES)
    sel = (head[None, :, None] == (jnp.arange(SSD_GROUPS)[:, None, None] * SSD_HPG + head[None, None, :])) \
        & (head[None, None, :] < SSD_HPG)
    sel = sel.astype(BF16)
    expand = ((jnp.arange(gw)[None, :] // SSD_HEADDIM) == head[:, None]).astype(BF16)
    pad = LANES - SSD_HEADS
    dtb = jnp.pad(dt_bias, (0, pad)).reshape(1, LANES)
    alog = jnp.pad(a_log, (0, pad)).reshape(1, LANES)
    dexp = jnp.repeat(d_skip, SSD_HEADDIM).reshape(1, SSD_INNER)
    cb2 = conv_b.reshape(1, SSD_CONV_CH)
    nw = norm_w.reshape(1, SSD_INNER)
    cxo, cbo, cco = 0, SSD_INNER // st, (SSD_INNER + SSD_GROUPS * st) // st
    in_specs = [
        pl.BlockSpec((1, TBLK, gw), lambda b, g, c: (b, c, xoff + g)),
        pl.BlockSpec((1, TBLK, st), lambda b, g, c: (b, c, boff + g)),
        pl.BlockSpec((1, TBLK, st), lambda b, g, c: (b, c, coff + g)),
        pl.BlockSpec((1, TBLK, gw), lambda b, g, c: (b, c, zoff + g)),
        pl.BlockSpec((1, TBLK, LANES), lambda b, g, c: (b, c, 0)),
        pl.BlockSpec((1, LANES, LANES), lambda b, g, c: (g, 0, 0)),
        pl.BlockSpec((LANES, gw), lambda b, g, c: (0, 0)),
        pl.BlockSpec((SSD_CONV, gw), lambda b, g, c: (0, cxo + g)),
        pl.BlockSpec((SSD_CONV, st), lambda b, g, c: (0, cbo + g)),
        pl.BlockSpec((SSD_CONV, st), lambda b, g, c: (0, cco + g)),
        pl.BlockSpec((1, gw), lambda b, g, c: (0, cxo + g)),
        pl.BlockSpec((1, st), lambda b, g, c: (0, cbo + g)),
        pl.BlockSpec((1, st), lambda b, g, c: (0, cco + g)),
        pl.BlockSpec((1, LANES), lambda b, g, c: (0, 0)),
        pl.BlockSpec((1, LANES), lambda b, g, c: (0, 0)),
        pl.BlockSpec((1, gw), lambda b, g, c: (0, g)),
        pl.BlockSpec((1, gw), lambda b, g, c: (0, g)),
    ]
    return pl.pallas_call(
        functools.partial(_ssd_kernel, seq_len=seq_len),
        grid=(bsz, SSD_GROUPS, nc),
        in_specs=in_specs,
        out_specs=pl.BlockSpec((1, TBLK, gw), lambda b, g, c: (b, c, g)),
        out_shape=jax.ShapeDtypeStruct((bsz, seq_len, SSD_INNER), BF16),
        scratch_shapes=[pltpu.VMEM((st, gw), F32),
                        pltpu.VMEM((8 + TBLK, gw), F32),
                        pltpu.VMEM((8 + TBLK, st), F32),
                        pltpu.VMEM((8 + TBLK, st), F32)],
        compiler_params=_cparams(("parallel", "parallel", "arbitrary"), 32),
        name="ssd_scan",
    )(p_main, p_main, p_main, p_main, p_small, sel, expand,
      conv_w, conv_w, conv_w, cb2, cb2, cb2, dtb, alog, dexp, nw)


def _fox_cum_kernel(p_ref, b_ref, o_ref, *, seq_len):
    nfull, tail = seq_len // TBLK, seq_len % TBLK
    r = lax.broadcasted_iota(jnp.int32, (TBLK, TBLK), 0)
    c = lax.broadcasted_iota(jnp.int32, (TBLK, TBLK), 1)
    upper = (r <= c).astype(F32)
    carry = jnp.zeros((LANES, 1), F32)
    for blk in range(nfull + (1 if tail else 0)):
        rows = TBLK if blk < nfull else tail
        lf = _log_sigmoid(p_ref[0, blk * TBLK:blk * TBLK + rows, :] + b_ref[...])
        if rows < TBLK:
            lf = jnp.concatenate([lf, jnp.zeros((TBLK - rows, LANES), F32)], axis=0)
        cs = jnp.dot(lf.T, upper, precision=HIGHEST, preferred_element_type=F32) + carry
        o_ref[0, :, blk * TBLK:(blk + 1) * TBLK] = cs
        carry = cs[:, TBLK - 1:TBLK]


def _fox_cum(p_small, fgate_b, *, f_lane):
    bsz, seq_len, _ = p_small.shape
    lp = pl.cdiv(seq_len, TBLK) * TBLK
    bias = jnp.zeros((LANES,), F32).at[f_lane:f_lane + FOX_HEADS].set(fgate_b).reshape(1, LANES)
    return pl.pallas_call(
        functools.partial(_fox_cum_kernel, seq_len=seq_len),
        grid=(bsz,),
        in_specs=[pl.BlockSpec((1, seq_len, LANES), lambda b: (b, 0, 0)),
                  pl.BlockSpec((1, LANES), lambda b: (0, 0))],
        out_specs=pl.BlockSpec((1, LANES, lp), lambda b: (b, 0, 0)),
        out_shape=jax.ShapeDtypeStruct((bsz, LANES, lp), F32),
        compiler_params=_cparams(("parallel",), 32),
        name="fox_cum",
    )(p_small, bias)


def _fox_kernel(q_ref, k_ref, v_ref, ck_ref, o_ref, *, seq_len):
    nfull, tail = seq_len // TBLK, seq_len % TBLK
    qi = pl.program_id(2)
    q = (q_ref[0] * (FOX_HEADDIM ** -0.5)).astype(BF16)

    def block(kb, vb, ckb, mask, carry):
        m, l, acc = carry
        s = lax.dot_general(q, kb, (((1,), (1,)), ((), ())), preferred_element_type=F32) - ckb
        if mask is not None:
            s = jnp.where(mask, s, NEG_INF)
        m_new = jnp.maximum(m, jnp.max(s, axis=-1, keepdims=True))
        a = jnp.exp(m - m_new)
        p = jnp.exp(s - m_new)
        l = a * l + jnp.sum(p, axis=-1, keepdims=True)
        acc = a * acc + jnp.dot(p.astype(BF16), vb, preferred_element_type=F32)
        return m_new, l, acc

    def body(j, carry):
        off = pl.multiple_of(j * TBLK, TBLK)
        kb = k_ref[0, pl.ds(off, TBLK), :].astype(BF16)
        vb = v_ref[0, pl.ds(off, TBLK), :].astype(BF16)
        ckb = ck_ref[0, 0, :, pl.ds(off, TBLK)]
        return block(kb, vb, ckb, None, carry)

    init = (jnp.full((TBLK, 1), NEG_INF, F32), jnp.zeros((TBLK, 1), F32), jnp.zeros((TBLK, FOX_HEADDIM), F32))
    carry = lax.fori_loop(0, jnp.minimum(qi, nfull), body, init)
    causal = _tril(TBLK)

    def finish(kb, vb, ckb):
        _, l, acc = block(kb, vb, ckb, causal, carry)
        o_ref[0] = (acc / l).astype(o_ref.dtype)

    @pl.when(qi < nfull)
    def _():
        off = pl.multiple_of(qi * TBLK, TBLK)
        finish(k_ref[0, pl.ds(off, TBLK), :].astype(BF16),
               v_ref[0, pl.ds(off, TBLK), :].astype(BF16),
               ck_ref[0, 0, :, pl.ds(off, TBLK)])

    if tail:
        @pl.when(qi == nfull)
        def _():
            zpad = jnp.zeros((TBLK - tail, FOX_HEADDIM), F32)
            kb = jnp.concatenate([k_ref[0, nfull * TBLK:seq_len, :], zpad], axis=0).astype(BF16)
            vb = jnp.concatenate([v_ref[0, nfull * TBLK:seq_len, :], zpad], axis=0).astype(BF16)
            finish(kb, vb, ck_ref[0, 0, :, nfull * TBLK:(nfull + 1) * TBLK])


def _fox(p_main, cum_t, *, q_col, f_lane):
    bsz, seq_len, _ = p_main.shape
    nq = pl.cdiv(seq_len, TBLK)
    lp = nq * TBLK
    qo = q_col // FOX_HEADDIM
    ko = qo + FOX_HEADS
    vo = ko + FOX_HEADS
    ck = cum_t.reshape(bsz, LANES, 1, lp)
    return pl.pallas_call(
        functools.partial(_fox_kernel, seq_len=seq_len),
        grid=(bsz, FOX_HEADS, nq),
        in_specs=[pl.BlockSpec((1, TBLK, FOX_HEADDIM), lambda b, h, i: (b, i, qo + h)),
                  pl.BlockSpec((1, seq_len, FOX_HEADDIM), lambda b, h, i: (b, 0, ko + h)),
                  pl.BlockSpec((1, seq_len, FOX_HEADDIM), lambda b, h, i: (b, 0, vo + h)),
                  pl.BlockSpec((1, 1, 1, lp), lambda b, h, i: (b, f_lane + h, 0, 0))],
        out_specs=pl.BlockSpec((1, TBLK, FOX_HEADDIM), lambda b, h, i: (b, i, h)),
        out_shape=jax.ShapeDtypeStruct((bsz, seq_len, FOX_WIDTH), BF16),
        compiler_params=_cparams(("parallel", "parallel", "arbitrary"), 32),
        name="fox_attn",
    )(p_main, p_main, p_main, ck)


GLA_TBLK = 256


def _gla_kernel(q_ref, k_ref, v_ref, r_ref, glr_ref, wgu_ref, gb_ref, nw_ref, o_ref, st_ref, *, seq_len):
    t = pl.program_id(2)

    @pl.when(t == 0)
    def _():
        st_ref[...] = jnp.zeros_like(st_ref)

    valid = seq_len - t * GLA_TBLK
    rmask = lax.broadcasted_iota(jnp.int32, (GLA_TBLK, 1), 0) < valid
    gpre = jnp.dot(glr_ref[0].astype(BF16), wgu_ref[...], preferred_element_type=F32) + gb_ref[...]
    g_all = jnp.where(rmask, _log_sigmoid(gpre) * (1.0 / GLA_TAU), 0.0)
    q_all = jnp.where(rmask, q_ref[0], 0.0) * (GLA_DK ** -0.5)
    k_all = jnp.where(rmask, k_ref[0], 0.0)
    v_all = jnp.where(rmask, v_ref[0], 0.0)
    tri = _tril(GLA_CHUNK)
    tri_f = tri.astype(F32)
    mid = GLA_CHUNK // 2
    outs = []
    for ci in range(GLA_TBLK // GLA_CHUNK):
        sl = slice(ci * GLA_CHUNK, (ci + 1) * GLA_CHUNK)
        qc, kc, vc, gc = q_all[sl], k_all[sl], v_all[sl], g_all[sl]
        vcb = vc.astype(BF16)
        bcum = jnp.dot(tri_f, gc, precision=HIGHEST, preferred_element_type=F32)
        bref = bcum[mid:mid + 1, :]
        btot = bcum[GLA_CHUNK - 1:GLA_CHUNK, :]
        qe = (qc * jnp.exp(bcum - bref)).astype(BF16)
        ke = (kc * jnp.exp(bref - bcum)).astype(BF16)
        scores = lax.dot_general(qe, ke, (((1,), (1,)), ((), ())), preferred_element_type=F32)
        scores = jnp.where(tri, scores, 0.0).astype(BF16)
        state_t = st_ref[...]
        qd = (qc * jnp.exp(bcum)).astype(BF16)
        o = jnp.dot(scores, vcb, preferred_element_type=F32) + lax.dot_general(
            qd, state_t.astype(BF16), (((1,), (1,)), ((), ())), preferred_element_type=F32)
        kd = (kc * jnp.exp(btot - bcum)).astype(BF16)
        st_ref[...] = state_t * jnp.exp(btot) + lax.dot_general(
            vcb, kd, (((0,), (0,)), ((), ())), preferred_element_type=F32)
        outs.append(o)
    o = jnp.concatenate(outs, axis=0)
    ms = jnp.mean(o * o, axis=-1, keepdims=True)
    o = o * lax.rsqrt(ms + LN_EPS) * nw_ref[...]
    o_ref[0] = (o * _silu(r_ref[0])).astype(o_ref.dtype)


def _gla(p_main, p_small, w_gate_up, gate_b, norm_w, *, q_col, k_col, v_col, r_col):
    bsz, seq_len, _ = p_main.shape
    nt = pl.cdiv(seq_len, GLA_TBLK)
    qo, ko, vo, ro = q_col // GLA_DK, k_col // GLA_DK, v_col // GLA_DV, r_col // GLA_DV
    wgu = jnp.pad(w_gate_up, ((0, LANES - GLA_RANK), (0, 0))).astype(BF16)
    return pl.pallas_call(
        functools.partial(_gla_kernel, seq_len=seq_len),
        grid=(bsz, GLA_HEADS, nt),
        in_specs=[pl.BlockSpec((1, GLA_TBLK, GLA_DK), lambda b, h, t: (b, t, qo + h)),
                  pl.BlockSpec((1, GLA_TBLK, GLA_DK), lambda b, h, t: (b, t, ko + h)),
                  pl.BlockSpec((1, GLA_TBLK, GLA_DV), lambda b, h, t: (b, t, vo + h)),
                  pl.BlockSpec((1, GLA_TBLK, GLA_DV), lambda b, h, t: (b, t, ro + h)),
                  pl.BlockSpec((1, GLA_TBLK, LANES), lambda b, h, t: (b, t, 0)),
                  pl.BlockSpec((LANES, GLA_DK), lambda b, h, t: (0, h)),
                  pl.BlockSpec((1, GLA_DK), lambda b, h, t: (0, h)),
                  pl.BlockSpec((1, GLA_DV), lambda b, h, t: (0, 0))],
        out_specs=pl.BlockSpec((1, GLA_TBLK, GLA_DV), lambda b, h, t: (b, t, h)),
        out_shape=jax.ShapeDtypeStruct((bsz, seq_len, GLA_VWIDTH), BF16),
        scratch_shapes=[pltpu.VMEM((GLA_DV, GLA_DK), F32)],
        compiler_params=_cparams(("parallel", "parallel", "arbitrary"), 32),
        name="gla_scan",
    )(p_main, p_main, p_main, p_main, p_small, wgu, gate_b.reshape(1, GLA_KWIDTH), norm_w.reshape(1, GLA_DV))


CONV_CBLK = 256


def _conf_kernel(ua_ref, ug_ref, w_ref, b_ref, g_ref, be_ref, o_ref, ext_ref, acc_ref, *, seq_len):
    t = pl.program_id(1)

    @pl.when(t == 0)
    def _():
        ext_ref[0:CONV_HALO, :] = jnp.zeros((CONV_HALO, CONV_CH), F32)

    valid = seq_len - t * TBLK
    rmask = lax.broadcasted_iota(jnp.int32, (TBLK, 1), 0) < valid
    glu = jnp.where(rmask, ua_ref[0] * _sigmoid(ug_ref[0]), 0.0)
    ext_ref[CONV_HALO:CONV_HALO + TBLK, :] = glu
    base = CONV_HALO - (CONV_WIDTH - 1)
    for cb in range(CONV_CH // CONV_CBLK):
        cs = slice(cb * CONV_CBLK, (cb + 1) * CONV_CBLK)
        acc = jnp.broadcast_to(b_ref[:, cs], (TBLK, CONV_CBLK))
        for k in range(CONV_WIDTH):
            acc = acc + w_ref[k:k + 1, cs] * ext_ref[base + k:base + k + TBLK, cs]
        acc_ref[:, cs] = acc
    ext_ref[0:CONV_HALO, :] = ext_ref[TBLK:TBLK + CONV_HALO, :]
    u = acc_ref[...]
    uc = u - jnp.mean(u, axis=-1, keepdims=True)
    var = jnp.mean(uc * uc, axis=-1, keepdims=True)
    o_ref[0] = _silu(uc * lax.rsqrt(var + LN_EPS) * g_ref[...] + be_ref[...]).astype(o_ref.dtype)


def _conformer_conv(p_main, dw_w, dw_b, ln_g, ln_b, *, glu_col):
    bsz, seq_len, _ = p_main.shape
    nt = pl.cdiv(seq_len, TBLK)
    ao = glu_col // CONV_CH
    vec = pl.BlockSpec((1, CONV_CH), lambda b, t: (0, 0))
    return pl.pallas_call(
        functools.partial(_conf_kernel, seq_len=seq_len),
        grid=(bsz, nt),
        in_specs=[pl.BlockSpec((1, TBLK, CONV_CH), lambda b, t: (b, t, ao)),
                  pl.BlockSpec((1, TBLK, CONV_CH), lambda b, t: (b, t, ao + 1)),
                  pl.BlockSpec((CONV_WIDTH, CONV_CH), lambda b, t: (0, 0)),
                  vec, vec, vec],
        out_specs=pl.BlockSpec((1, TBLK, CONV_CH), lambda b, t: (b, t, 0)),
        out_shape=jax.ShapeDtypeStruct((bsz, seq_len, CONV_CH), BF16),
        scratch_shapes=[pltpu.VMEM((CONV_HALO + TBLK, CONV_CH), F32),
                        pltpu.VMEM((TBLK, CONV_CH), F32)],
        compiler_params=_cparams(("parallel", "arbitrary"), 32),
        name="conformer_conv",
    )(p_main, p_main, dw_w, dw_b.reshape(1, CONV_CH), ln_g.reshape(1, CONV_CH), ln_b.reshape(1, CONV_CH))


EV_XBC_COL = 0
EV_Q_COL = SSD_CONV_CH
EV_Z_COL = SSD_CONV_CH + 3 * FOX_WIDTH
EV_F_LANE = SSD_HEADS
OD_Q_COL = 0
OD_K_COL = GLA_KWIDTH
OD_V_COL = 2 * GLA_KWIDTH
OD_R_COL = 2 * GLA_KWIDTH + GLA_VWIDTH
OD_GLU_COL = 2 * GLA_KWIDTH + 2 * GLA_VWIDTH


def _even_weights(w_in):
    z0, x0 = 0, SSD_INNER
    d0 = x0 + SSD_CONV_CH
    q0 = d0 + SSD_HEADS
    f0 = q0 + 3 * FOX_WIDTH
    main = jnp.concatenate([w_in[:, x0:d0], w_in[:, q0:f0], w_in[:, z0:x0]], axis=1).astype(BF16)
    small = jnp.concatenate([w_in[:, d0:q0], w_in[:, f0:f0 + FOX_HEADS],
                             jnp.zeros((D_MODEL, LANES - SSD_HEADS - FOX_HEADS), w_in.dtype)], axis=1).astype(BF16)
    return main, small


def _odd_weights(w_in):
    g0 = 2 * GLA_KWIDTH + GLA_VWIDTH
    r0 = g0 + GLA_RANK
    main = jnp.concatenate([w_in[:, 0:g0], w_in[:, r0:]], axis=1).astype(BF16)
    small = jnp.concatenate([w_in[:, g0:r0], jnp.zeros((D_MODEL, LANES - GLA_RANK), w_in.dtype)], axis=1).astype(BF16)
    return main, small


def _in_proj(hb, w_main, w_small, bsz, seq_len, name):
    p_main = _matmul(hb, w_main, bm=1376, bn=512, out_dtype=F32, vmem_mib=48, name=name + "_main")
    p_small = _matmul(hb, w_small, bm=1376, bn=LANES, out_dtype=F32, vmem_mib=48, name=name + "_small")
    return p_main.reshape(bsz, seq_len, -1), p_small.reshape(bsz, seq_len, LANES)


def _out_proj_ln(a, c, w_out, h, g, b, alpha, name):
    m = h.shape[0]
    k1 = a.shape[-1]
    wb = w_out.astype(BF16)
    mix = _matmul2(a.reshape(m, k1), c.reshape(m, -1), wb[:k1], wb[k1:], bm=688, bn=512, vmem_mib=48, name=name)
    return _add_ln(h, mix, g, b, alpha=alpha, name=name + "_ln")


def _ffn_ln(h, hb, w_gate, w_up, w_down, g, b, alpha, name):
    mid = _swiglu_up(hb, w_gate.astype(BF16), w_up.astype(BF16), bm=1376, bn=256, vmem_mib=48, name=name + "_up")
    down = _matmul(mid, w_down.astype(BF16), bm=688, bn=256, out_dtype=F32, vmem_mib=56, name=name + "_down")
    return _add_ln(h, down, g, b, alpha=alpha, name=name + "_ln")


def kernel(x, meta_tokens, ev_w_in, ev_conv_w, ev_conv_b, ev_dt_bias, ev_a_log, ev_d_skip, ev_ssm_norm_w, ev_fgate_b, ev_w_out, od_w_in, od_w_gate_up, od_gate_b, od_gla_norm_w, od_dwconv_w, od_dwconv_b, od_conv_ln_g, od_conv_ln_b, od_w_out, ln_mix_g, ln_mix_b, ffn_w_gate, ffn_w_up, ffn_w_down, ln_ffn_g, ln_ffn_b):
    alpha = (2.0 * DEPTH) ** 0.25
    bsz = x.shape[0]
    meta = jnp.broadcast_to(meta_tokens[None].astype(x.dtype), (bsz, N_META, D_MODEL))
    h3 = jnp.concatenate([meta, x], axis=1)
    seq_len = h3.shape[1]
    h = h3.reshape(bsz * seq_len, D_MODEL)
    hb = h.astype(BF16)
    for i in range(DEPTH):
        j = i // 2
        if i % 2 == 0:
            w_main, w_small = _even_weights(ev_w_in[j])
            p_main, p_small = _in_proj(hb, w_main, w_small, bsz, seq_len, "ev_in")
            y = _ssd(p_main, p_small, ev_conv_w[j], ev_conv_b[j], ev_dt_bias[j], ev_a_log[j], ev_d_skip[j],
                     ev_ssm_norm_w[j], xbc_col=EV_XBC_COL, z_col=EV_Z_COL)
            cum_t = _fox_cum(p_small, ev_fgate_b[j], f_lane=EV_F_LANE)
            o = _fox(p_main, cum_t, q_col=EV_Q_COL, f_lane=EV_F_LANE)
            h, hb = _out_proj_ln(y, o, ev_w_out[j], h, ln_mix_g[i], ln_mix_b[i], alpha, "ev_out")
        else:
            w_main, w_small = _odd_weights(od_w_in[j])
            p_main, p_small = _in_proj(hb, w_main, w_small, bsz, seq_len, "od_in")
            o = _gla(p_main, p_small, od_w_gate_up[j], od_gate_b[j], od_gla_norm_w[j],
                     q_col=OD_Q_COL, k_col=OD_K_COL, v_col=OD_V_COL, r_col=OD_R_COL)
            u = _conformer_conv(p_main, od_dwconv_w[j], od_dwconv_b[j], od_conv_ln_g[j], od_conv_ln_b[j],
                                glu_col=OD_GLU_COL)
            h, hb = _out_proj_ln(o, u, od_w_out[j], h, ln_mix_g[i], ln_mix_b[i], alpha, "od_out")
        h, hb = _ffn_ln(h, hb, ffn_w_gate[i], ffn_w_up[i], ffn_w_down[i], ln_ffn_g[i], ln_ffn_b[i], alpha, "ffn")
    return h.reshape(bsz, seq_len, D_MODEL)[:, N_META:]
```

```python
import functools

import jax
import jax.numpy as jnp
from jax import lax
from jax.experimental import pallas as pl
from jax.experimental.pallas import tpu as pltpu

F32 = jnp.float32
BF16 = jnp.bfloat16
HIGHEST = lax.Precision.HIGHEST

D_MODEL = 4096
DEPTH = 2
N_META = 16
SSD_HEADDIM = 64
SSD_INNER = D_MODEL
SSD_HEADS = SSD_INNER // SSD_HEADDIM
SSD_GROUPS = 8
SSD_HPG = SSD_HEADS // SSD_GROUPS
SSD_STATE = 128
SSD_CONV = 4
SSD_GROUP_W = SSD_HPG * SSD_HEADDIM
SSD_CONV_CH = SSD_INNER + 2 * SSD_GROUPS * SSD_STATE
FOX_HEADDIM = 128
FOX_WIDTH = D_MODEL // 2
FOX_HEADS = FOX_WIDTH // FOX_HEADDIM
NEG_INF = -1e30
GLA_HEADS = D_MODEL // 512
GLA_DK = 256
GLA_DV = 512
GLA_KWIDTH = GLA_HEADS * GLA_DK
GLA_VWIDTH = GLA_HEADS * GLA_DV
GLA_RANK = 16
GLA_TAU = 16.0
GLA_CHUNK = 64
CONV_CH = D_MODEL // 2
CONV_WIDTH = 31
D_FF = -(-8 * D_MODEL // (3 * 256)) * 256
LN_EPS = 1e-5

LANES = 128
TBLK = 128
CONV_HALO = 32
MIB = 1024 * 1024


def _cparams(semantics, vmem_mib):
    return pltpu.CompilerParams(dimension_semantics=semantics, vmem_limit_bytes=vmem_mib * MIB)


def _tile(dim, target):
    return dim if dim < target else target


def _sigmoid(x):
    return 1.0 / (1.0 + jnp.exp(-x))


def _silu(x):
    return x * _sigmoid(x)


def _softplus(x):
    return jnp.maximum(x, 0.0) + jnp.log1p(jnp.exp(-jnp.abs(x)))


def _log_sigmoid(x):
    return jnp.minimum(x, 0.0) - jnp.log1p(jnp.exp(-jnp.abs(x)))


def _split_dot(v, onehot, parts):
    out = None
    rem = v
    for _ in range(parts):
        piece = rem.astype(BF16)
        term = jnp.dot(piece, onehot, preferred_element_type=F32)
        out = term if out is None else out + term
        rem = rem - piece.astype(F32)
    return out


def _tril(n):
    r = lax.broadcasted_iota(jnp.int32, (n, n), 0)
    c = lax.broadcasted_iota(jnp.int32, (n, n), 1)
    return c <= r


def _mm_kernel(x_ref, w_ref, o_ref):
    o_ref[...] = jnp.dot(x_ref[...], w_ref[...].astype(BF16), preferred_element_type=F32).astype(o_ref.dtype)


def _matmul(x, w, layer, *, n_out, bm, bn, out_dtype, vmem_mib, name):
    m, k = x.shape
    bm = _tile(m, bm)
    return pl.pallas_call(
        _mm_kernel,
        grid=(pl.cdiv(m, bm), n_out // bn),
        in_specs=[pl.BlockSpec((bm, k), lambda i, j: (i, 0)),
                  pl.BlockSpec((None, k, bn), lambda i, j: (layer, 0, j))],
        out_specs=pl.BlockSpec((bm, bn), lambda i, j: (i, j)),
        out_shape=jax.ShapeDtypeStruct((m, n_out), out_dtype),
        compiler_params=_cparams(("parallel", "arbitrary"), vmem_mib),
        name=name,
    )(x, w)


def _mm2_kernel(a_ref, c_ref, w1_ref, w2_ref, o_ref):
    acc = jnp.dot(a_ref[...], w1_ref[...].astype(BF16), preferred_element_type=F32)
    acc = acc + jnp.dot(c_ref[...], w2_ref[...].astype(BF16), preferred_element_type=F32)
    o_ref[...] = acc.astype(o_ref.dtype)


def _matmul2(a, c, w, layer, *, bm, bn, vmem_mib, name):
    m, k1 = a.shape
    k2 = c.shape[1]
    n = w.shape[2]
    bm = _tile(m, bm)
    return pl.pallas_call(
        _mm2_kernel,
        grid=(pl.cdiv(m, bm), n // bn),
        in_specs=[pl.BlockSpec((bm, k1), lambda i, j: (i, 0)),
                  pl.BlockSpec((bm, k2), lambda i, j: (i, 0)),
                  pl.BlockSpec((None, k1, bn), lambda i, j: (layer, 0, j)),
                  pl.BlockSpec((None, k2, bn), lambda i, j: (layer, k1 // k2, j))],
        out_specs=pl.BlockSpec((bm, bn), lambda i, j: (i, j)),
        out_shape=jax.ShapeDtypeStruct((m, n), F32),
        compiler_params=_cparams(("parallel", "arbitrary"), vmem_mib),
        name=name,
    )(a, c, w, w)


def _swiglu_up_kernel(x_ref, wg_ref, wu_ref, o_ref):
    x = x_ref[...]
    g = jnp.dot(x, wg_ref[...].astype(BF16), preferred_element_type=F32)
    u = jnp.dot(x, wu_ref[...].astype(BF16), preferred_element_type=F32)
    o_ref[...] = (_silu(g) * u).astype(o_ref.dtype)


def _swiglu_up(x, wg, wu, layer, *, bm, bn, vmem_mib, name):
    m, k = x.shape
    n = wg.shape[2]
    bm = _tile(m, bm)
    wspec = pl.BlockSpec((None, k, bn), lambda i, j: (layer, 0, j))
    return pl.pallas_call(
        _swiglu_up_kernel,
        grid=(pl.cdiv(m, bm), n // bn),
        in_specs=[pl.BlockSpec((bm, k), lambda i, j: (i, 0)), wspec, wspec],
        out_specs=pl.BlockSpec((bm, bn), lambda i, j: (i, j)),
        out_shape=jax.ShapeDtypeStruct((m, n), BF16),
        compiler_params=_cparams(("parallel", "arbitrary"), vmem_mib),
        name=name,
    )(x, wg, wu)


def _add_ln_kernel(h_ref, m_ref, g_ref, b_ref, of_ref, ob_ref, *, alpha):
    y = alpha * h_ref[...] + m_ref[...]
    yc = y - jnp.mean(y, axis=-1, keepdims=True)
    var = jnp.mean(yc * yc, axis=-1, keepdims=True)
    out = yc * lax.rsqrt(var + LN_EPS) * g_ref[...] + b_ref[...]
    of_ref[...] = out
    ob_ref[...] = out.astype(BF16)


def _add_ln(h, mix, g, b, *, alpha, name):
    m, d = h.shape
    bm = _tile(m, 128)
    row = pl.BlockSpec((bm, d), lambda i: (i, 0))
    vec = pl.BlockSpec((1, d), lambda i: (0, 0))
    return pl.pallas_call(
        functools.partial(_add_ln_kernel, alpha=alpha),
        grid=(pl.cdiv(m, bm),),
        in_specs=[row, row, vec, vec],
        out_specs=[row, row],
        out_shape=[jax.ShapeDtypeStruct((m, d), F32), jax.ShapeDtypeStruct((m, d), BF16)],
        compiler_params=_cparams(("parallel",), 32),
        name=name,
    )(h, mix, g.reshape(1, d), b.reshape(1, d))


def _ssd_kernel(xs_ref, bm_ref, cm_ref, z_ref, dtf_ref, sel_ref, e_ref,
                cwx_ref, cwb_ref, cwc_ref, cbx_ref, cbb_ref, cbc_ref,
                dtb_ref, alog_ref, dexp_ref, nw_ref, y_ref,
                s_ref, ex_ref, eb_ref, ec_ref, *, seq_len):
    c = pl.program_id(2)

    @pl.when(c == 0)
    def _():
        s_ref[...] = jnp.zeros_like(s_ref)
        ex_ref[0:8, :] = jnp.zeros((8, SSD_GROUP_W), F32)
        eb_ref[0:8, :] = jnp.zeros((8, SSD_STATE), F32)
        ec_ref[0:8, :] = jnp.zeros((8, SSD_STATE), F32)

    valid = seq_len - c * TBLK
    rmask = lax.broadcasted_iota(jnp.int32, (TBLK, 1), 0) < valid

    def conv_silu(raw_ref, ext_ref, w_ref, bias_ref):
        raw = jnp.where(rmask, raw_ref[0], 0.0)
        ext_ref[8:8 + TBLK, :] = raw
        acc = bias_ref[...] + w_ref[SSD_CONV - 1:SSD_CONV, :] * raw
        for k in range(SSD_CONV - 1):
            lo = 8 - (SSD_CONV - 1) + k
            acc = acc + w_ref[k:k + 1, :] * ext_ref[lo:lo + TBLK, :]
        ext_ref[0:8, :] = raw[TBLK - 8:TBLK, :]
        return _silu(acc)

    xs = conv_silu(xs_ref, ex_ref, cwx_ref, cbx_ref)
    bmat = conv_silu(bm_ref, eb_ref, cwb_ref, cbb_ref)
    cmat = conv_silu(cm_ref, ec_ref, cwc_ref, cbc_ref)

    dt_all = jnp.where(rmask, _softplus(dtf_ref[0] + dtb_ref[...]), 0.0)
    la_all = dt_all * (-jnp.exp(alog_ref[...]))
    sel = sel_ref[0]
    dt_g = _split_dot(dt_all, sel, 3)
    la_g = _split_dot(la_all, sel, 3)
    tri = _tril(TBLK)
    a_cs = jnp.dot(tri.astype(F32), la_g, precision=HIGHEST, preferred_element_type=F32)
    a_cs_t = a_cs.T
    e_in = jnp.exp(a_cs)
    e_out = jnp.exp(a_cs[TBLK - 1:TBLK, :] - a_cs)
    expanded = _split_dot(jnp.concatenate([dt_g, e_in, e_out], axis=0), e_ref[...], 2)
    dt_e = expanded[0:TBLK]
    ein_e = expanded[TBLK:2 * TBLK]
    eout_e = expanded[2 * TBLK:3 * TBLK]

    xdt = xs * dt_e
    bb = bmat.astype(BF16)
    cc = cmat.astype(BF16)
    state = s_ref[...]
    y = jnp.dot(cc, state.astype(BF16), preferred_element_type=F32) * ein_e
    cb = lax.dot_general(cc, bb, (((1,), (1,)), ((), ())), preferred_element_type=F32)
    lane = lax.broadcasted_iota(jnp.int32, (TBLK, LANES), 1)
    cols = []
    for m in range(SSD_GROUP_W // LANES):
        xcol = xdt[:, m * LANES:(m + 1) * LANES]
        ycol = None
        for half in range(LANES // SSD_HEADDIM):
            r = m * (LANES // SSD_HEADDIM) + half
            colb = jnp.broadcast_to(a_cs[:, r:r + 1], (TBLK, TBLK))
            rowb = jnp.broadcast_to(a_cs_t[r:r + 1, :], (TBLK, TBLK))
            lmat = jnp.where(tri, jnp.exp(colb - rowb), 0.0) * cb
            in_half = (lane >= half * SSD_HEADDIM) & (lane < (half + 1) * SSD_HEADDIM)
            xh = jnp.where(in_half, xcol, 0.0).astype(BF16)
            term = jnp.dot(lmat.astype(BF16), xh, preferred_element_type=F32)
            ycol = term if ycol is None else ycol + term
        cols.append(ycol)
    y = y + jnp.concatenate(cols, axis=1) + xs * dexp_ref[...]

    xd = (xdt * eout_e).astype(BF16)
    s_ref[...] = state * ein_e[TBLK - 1:TBLK, :] + lax.dot_general(
        bb, xd, (((0,), (0,)), ((), ())), preferred_element_type=F32)

    yz = y * _silu(z_ref[0])
    ms = jnp.mean(yz * yz, axis=-1, keepdims=True)
    y_ref[0] = (yz * lax.rsqrt(ms + LN_EPS) * nw_ref[...]).astype(y_ref.dtype)


def _ssd(p, conv_w, conv_b, dt_bias, a_log, d_skip, norm_w, *, xbc_col, z_col, dt_col):
    bsz, seq_len, _ = p.shape
    nc = pl.cdiv(seq_len, TBLK)
    gw, st = SSD_GROUP_W, SSD_STATE
    xoff, boff, coff = xbc_col // gw, (xbc_col + SSD_INNER) // st, (xbc_col + SSD_INNER + SSD_GROUPS * st) // st
    zoff, doff = z_col // gw, dt_col // LANES
    head = jnp.arange(LANES)
    sel = (head[None, :, None] == (jnp.arange(SSD_GROUPS)[:, None, None] * SSD_HPG + head[None, None, :])) \
        & (head[None, None, :] < SSD_HPG)
    sel = sel.astype(BF16)
    expand = ((jnp.arange(gw)[None, :] // SSD_HEADDIM) == head[:, None]).astype(BF16)
    pad = LANES - SSD_HEADS
    dtb = jnp.pad(dt_bias, (0, pad)).reshape(1, LANES)
    alog = jnp.pad(a_log, (0, pad)).reshape(1, LANES)
    dexp = jnp.repeat(d_skip, SSD_HEADDIM).reshape(1, SSD_INNER)
    cb2 = conv_b.reshape(1, SSD_CONV_CH)
    nw = norm_w.reshape(1, SSD_INNER)
    cxo, cbo, cco = 0, SSD_INNER // st, (SSD_INNER + SSD_GROUPS * st) // st
    in_specs = [
        pl.BlockSpec((1, TBLK, gw), lambda b, g, c: (b, c, xoff + g)),
        pl.BlockSpec((1, TBLK, st), lambda b, g, c: (b, c, boff + g)),
        pl.BlockSpec((1, TBLK, st), lambda b, g, c: (b, c, coff + g)),
        pl.BlockSpec((1, TBLK, gw), lambda b, g, c: (b, c, zoff + g)),
        pl.BlockSpec((1, TBLK, LANES), lambda b, g, c: (b, c, doff)),
        pl.BlockSpec((1, LANES, LANES), lambda b, g, c: (g, 0, 0)),
        pl.BlockSpec((LANES, gw), lambda b, g, c: (0, 0)),
        pl.BlockSpec((SSD_CONV, gw), lambda b, g, c: (0, cxo + g)),
        pl.BlockSpec((SSD_CONV, st), lambda b, g, c: (0, cbo + g)),
        pl.BlockSpec((SSD_CONV, st), lambda b, g, c: (0, cco + g)),
        pl.BlockSpec((1, gw), lambda b, g, c: (0, cxo + g)),
        pl.BlockSpec((1, st), lambda b, g, c: (0, cbo + g)),
        pl.BlockSpec((1, st), lambda b, g, c: (0, cco + g)),
        pl.BlockSpec((1, LANES), lambda b, g, c: (0, 0)),
        pl.BlockSpec((1, LANES), lambda b, g, c: (0, 0)),
        pl.BlockSpec((1, gw), lambda b, g, c: (0, g)),
        pl.BlockSpec((1, gw), lambda b, g, c: (0, g)),
    ]
    return pl.pallas_call(
        functools.partial(_ssd_kernel, seq_len=seq_len),
        grid=(bsz, SSD_GROUPS, nc),
        in_specs=in_specs,
        out_specs=pl.BlockSpec((1, TBLK, gw), lambda b, g, c: (b, c, g)),
        out_shape=jax.ShapeDtypeStruct((bsz, seq_len, SSD_INNER), BF16),
        scratch_shapes=[pltpu.VMEM((st, gw), F32),
                        pltpu.VMEM((8 + TBLK, gw), F32),
                        pltpu.VMEM((8 + TBLK, st), F32),
                        pltpu.VMEM((8 + TBLK, st), F32)],
        compiler_params=_cparams(("parallel", "parallel", "arbitrary"), 32),
        name="ssd_scan",
    )(p, p, p, p, p, sel, expand, conv_w, conv_w, conv_w, cb2, cb2, cb2, dtb, alog, dexp, nw)


FOX_QTILE = 512


def _fox_cum_kernel(p_ref, b_ref, o_ref, *, seq_len):
    nfull, tail = seq_len // TBLK, seq_len % TBLK
    r = lax.broadcasted_iota(jnp.int32, (TBLK, TBLK), 0)
    c = lax.broadcasted_iota(jnp.int32, (TBLK, TBLK), 1)
    upper = (r <= c).astype(F32)
    carry = jnp.zeros((LANES, 1), F32)
    for blk in range(nfull + (1 if tail else 0)):
        rows = TBLK if blk < nfull else tail
        lf = _log_sigmoid(p_ref[0, blk * TBLK:blk * TBLK + rows, :] + b_ref[...])
        if rows < TBLK:
            lf = jnp.concatenate([lf, jnp.zeros((TBLK - rows, LANES), F32)], axis=0)
        cs = jnp.dot(lf.T, upper, precision=HIGHEST, preferred_element_type=F32) + carry
        o_ref[0, :, blk * TBLK:(blk + 1) * TBLK] = cs
        carry = cs[:, TBLK - 1:TBLK]


def _fox_cum(p, fgate_b, *, f_col):
    bsz, seq_len, _ = p.shape
    lp = pl.cdiv(seq_len, TBLK) * TBLK
    foff = f_col // LANES
    bias = jnp.pad(fgate_b, (0, LANES - FOX_HEADS)).reshape(1, LANES)
    return pl.pallas_call(
        functools.partial(_fox_cum_kernel, seq_len=seq_len),
        grid=(bsz,),
        in_specs=[pl.BlockSpec((1, seq_len, LANES), lambda b: (b, 0, foff)),
                  pl.BlockSpec((1, LANES), lambda b: (0, 0))],
        out_specs=pl.BlockSpec((1, LANES, lp), lambda b: (b, 0, 0)),
        out_shape=jax.ShapeDtypeStruct((bsz, LANES, lp), F32),
        compiler_params=_cparams(("parallel",), 32),
        name="fox_cum",
    )(p, bias)


def _fox_tiles(seq_len):
    aligned = (seq_len // TBLK) * TBLK
    tiles, start = [], 0
    while start < aligned:
        rows = min(FOX_QTILE, aligned - start)
        tiles.append((start, rows, start + rows))
        start += rows
    if seq_len > aligned:
        tiles.append((aligned, seq_len - aligned, aligned + TBLK))
    return tiles


def _fox_kernel(q_ref, k_ref, v_ref, ck_ref, o_ref, kt_ref, vb_ref, *, seq_len):
    for lo in range(0, seq_len, TBLK):
        rows = min(TBLK, seq_len - lo)
        kb = k_ref[0, lo:lo + rows, :]
        vb = v_ref[0, lo:lo + rows, :]
        if rows < TBLK:
            zpad = jnp.zeros((TBLK - rows, FOX_HEADDIM), F32)
            kb = jnp.concatenate([kb, zpad], axis=0)
            vb = jnp.concatenate([vb, zpad], axis=0)
        kt_ref[:, lo:lo + TBLK] = kb.T.astype(BF16)
        vb_ref[lo:lo + TBLK, :] = vb.astype(BF16)

    for start, rows, lk in _fox_tiles(seq_len):
        q = (q_ref[0, start:start + rows, :] * (FOX_HEADDIM ** -0.5)).astype(BF16)
        dw = lk - start
        r = lax.broadcasted_iota(jnp.int32, (rows, dw), 0)
        c = lax.broadcasted_iota(jnp.int32, (rows, dw), 1)
        sd = jnp.dot(q, kt_ref[:, start:lk], preferred_element_type=F32) - ck_ref[0, 0, :, start:lk]
        sd = jnp.where(c <= r, sd, NEG_INF)
        m = jnp.max(sd, axis=-1, keepdims=True)
        if start:
            so = jnp.dot(q, kt_ref[:, 0:start], preferred_element_type=F32) - ck_ref[0, 0, :, 0:start]
            m = jnp.maximum(m, jnp.max(so, axis=-1, keepdims=True))
        pd = jnp.exp(sd - m)
        l = jnp.sum(pd, axis=-1, keepdims=True)
        o = jnp.dot(pd.astype(BF16), vb_ref[start:lk, :], preferred_element_type=F32)
        if start:
            po = jnp.exp(so - m)
            l = l + jnp.sum(po, axis=-1, keepdims=True)
            o = o + jnp.dot(po.astype(BF16), vb_ref[0:start, :], preferred_element_type=F32)
        o_ref[0, start:start + rows, :] = (o / l).astype(o_ref.dtype)


def _fox(p, cum_t, *, q_col):
    bsz, seq_len, _ = p.shape
    lp = cum_t.shape[2]
    qo = q_col // FOX_HEADDIM
    ko = qo + FOX_HEADS
    vo = ko + FOX_HEADS
    ck = cum_t.reshape(bsz, LANES, 1, lp)
    return pl.pallas_call(
        functools.partial(_fox_kernel, seq_len=seq_len),
        grid=(bsz, FOX_HEADS),
        in_specs=[pl.BlockSpec((1, seq_len, FOX_HEADDIM), lambda b, h: (b, 0, qo + h)),
                  pl.BlockSpec((1, seq_len, FOX_HEADDIM), lambda b, h: (b, 0, ko + h)),
                  pl.BlockSpec((1, seq_len, FOX_HEADDIM), lambda b, h: (b, 0, vo + h)),
                  pl.BlockSpec((1, 1, 1, lp), lambda b, h: (b, h, 0, 0))],
        out_specs=pl.BlockSpec((1, seq_len, FOX_HEADDIM), lambda b, h: (b, 0, h)),
        out_shape=jax.ShapeDtypeStruct((bsz, seq_len, FOX_WIDTH), BF16),
        scratch_shapes=[pltpu.VMEM((FOX_HEADDIM, lp), BF16),
                        pltpu.VMEM((lp, FOX_HEADDIM), BF16)],
        compiler_params=_cparams(("parallel", "parallel"), 48),
        name="fox_attn",
    )(p, p, p, ck)


GLA_TBLK = 256


def _gla_kernel(q_ref, k_ref, v_ref, r_ref, glr_ref, wgu_ref, gb_ref, nw_ref, o_ref, st_ref, *, seq_len):
    t = pl.program_id(2)

    @pl.when(t == 0)
    def _():
        st_ref[...] = jnp.zeros_like(st_ref)

    valid = seq_len - t * GLA_TBLK
    rmask = lax.broadcasted_iota(jnp.int32, (GLA_TBLK, 1), 0) < valid
    gpre = jnp.dot(glr_ref[0].astype(BF16), wgu_ref[...], preferred_element_type=F32) + gb_ref[...]
    g_all = jnp.where(rmask, _log_sigmoid(gpre) * (1.0 / GLA_TAU), 0.0)
    q_all = jnp.where(rmask, q_ref[0], 0.0) * (GLA_DK ** -0.5)
    k_all = jnp.where(rmask, k_ref[0], 0.0)
    v_all = jnp.where(rmask, v_ref[0], 0.0)
    tri = _tril(GLA_CHUNK)
    tri_f = tri.astype(F32)
    mid = GLA_CHUNK // 2
    outs = []
    for ci in range(GLA_TBLK // GLA_CHUNK):
        sl = slice(ci * GLA_CHUNK, (ci + 1) * GLA_CHUNK)
        qc, kc, vc, gc = q_all[sl], k_all[sl], v_all[sl], g_all[sl]
        vcb = vc.astype(BF16)
        bcum = jnp.dot(tri_f, gc, precision=HIGHEST, preferred_element_type=F32)
        bref = bcum[mid:mid + 1, :]
        btot = bcum[GLA_CHUNK - 1:GLA_CHUNK, :]
        qe = (qc * jnp.exp(bcum - bref)).astype(BF16)
        ke = (kc * jnp.exp(bref - bcum)).astype(BF16)
        scores = lax.dot_general(qe, ke, (((1,), (1,)), ((), ())), preferred_element_type=F32)
        scores = jnp.where(tri, scores, 0.0).astype(BF16)
        state_t = st_ref[...]
        qd = (qc * jnp.exp(bcum)).astype(BF16)
        o = jnp.dot(scores, vcb, preferred_element_type=F32) + lax.dot_general(
            qd, state_t.astype(BF16), (((1,), (1,)), ((), ())), preferred_element_type=F32)
        kd = (kc * jnp.exp(btot - bcum)).astype(BF16)
        st_ref[...] = state_t * jnp.exp(btot) + lax.dot_general(
            vcb, kd, (((0,), (0,)), ((), ())), preferred_element_type=F32)
        outs.append(o)
    o = jnp.concatenate(outs, axis=0)
    ms = jnp.mean(o * o, axis=-1, keepdims=True)
    o = o * lax.rsqrt(ms + LN_EPS) * nw_ref[...]
    o_ref[0] = (o * _silu(r_ref[0])).astype(o_ref.dtype)


def _gla(p, pr, w_gate_up, gate_b, norm_w, *, q_col, k_col, v_col, glr_col, r_col):
    bsz, seq_len, _ = p.shape
    nt = pl.cdiv(seq_len, GLA_TBLK)
    qo, ko, vo, ro, go = q_col // GLA_DK, k_col // GLA_DK, v_col // GLA_DV, r_col // GLA_DV, glr_col // LANES
    wgu = jnp.pad(w_gate_up, ((0, LANES - GLA_RANK), (0, 0))).astype(BF16)
    return pl.pallas_call(
        functools.partial(_gla_kernel, seq_len=seq_len),
        grid=(bsz, GLA_HEADS, nt),
        in_specs=[pl.BlockSpec((1, GLA_TBLK, GLA_DK), lambda b, h, t: (b, t, qo + h)),
                  pl.BlockSpec((1, GLA_TBLK, GLA_DK), lambda b, h, t: (b, t, ko + h)),
                  pl.BlockSpec((1, GLA_TBLK, GLA_DV), lambda b, h, t: (b, t, vo + h)),
                  pl.BlockSpec((1, GLA_TBLK, GLA_DV), lambda b, h, t: (b, t, ro + h)),
                  pl.BlockSpec((1, GLA_TBLK, LANES), lambda b, h, t: (b, t, go)),
                  pl.BlockSpec((LANES, GLA_DK), lambda b, h, t: (0, h)),
                  pl.BlockSpec((1, GLA_DK), lambda b, h, t: (0, h)),
                  pl.BlockSpec((1, GLA_DV), lambda b, h, t: (0, 0))],
        out_specs=pl.BlockSpec((1, GLA_TBLK, GLA_DV), lambda b, h, t: (b, t, h)),
        out_shape=jax.ShapeDtypeStruct((bsz, seq_len, GLA_VWIDTH), BF16),
        scratch_shapes=[pltpu.VMEM((GLA_DV, GLA_DK), F32)],
        compiler_params=_cparams(("parallel", "parallel", "arbitrary"), 32),
        name="gla_scan",
    )(p, p, p, pr, p, wgu, gate_b.reshape(1, GLA_KWIDTH), norm_w.reshape(1, GLA_DV))


CONV_CBLK = 256


def _conf_kernel(ua_ref, ug_ref, w_ref, b_ref, g_ref, be_ref, o_ref, ext_ref, acc_ref, *, seq_len):
    t = pl.program_id(1)

    @pl.when(t == 0)
    def _():
        ext_ref[0:CONV_HALO, :] = jnp.zeros((CONV_HALO, CONV_CH), F32)

    valid = seq_len - t * TBLK
    rmask = lax.broadcasted_iota(jnp.int32, (TBLK, 1), 0) < valid
    glu = jnp.where(rmask, ua_ref[0] * _sigmoid(ug_ref[0]), 0.0)
    ext_ref[CONV_HALO:CONV_HALO + TBLK, :] = glu
    base = CONV_HALO - (CONV_WIDTH - 1)
    for cb in range(CONV_CH // CONV_CBLK):
        cs = slice(cb * CONV_CBLK, (cb + 1) * CONV_CBLK)
        acc = jnp.broadcast_to(b_ref[:, cs], (TBLK, CONV_CBLK))
        for k in range(CONV_WIDTH):
            acc = acc + w_ref[k:k + 1, cs] * ext_ref[base + k:base + k + TBLK, cs]
        acc_ref[:, cs] = acc
    ext_ref[0:CONV_HALO, :] = ext_ref[TBLK:TBLK + CONV_HALO, :]
    u = acc_ref[...]
    uc = u - jnp.mean(u, axis=-1, keepdims=True)
    var = jnp.mean(uc * uc, axis=-1, keepdims=True)
    o_ref[0] = _silu(uc * lax.rsqrt(var + LN_EPS) * g_ref[...] + be_ref[...]).astype(o_ref.dtype)


def _conformer_conv(p, dw_w, dw_b, ln_g, ln_b, *, glu_col):
    bsz, seq_len, _ = p.shape
    nt = pl.cdiv(seq_len, TBLK)
    ao = glu_col // CONV_CH
    vec = pl.BlockSpec((1, CONV_CH), lambda b, t: (0, 0))
    return pl.pallas_call(
        functools.partial(_conf_kernel, seq_len=seq_len),
        grid=(bsz, nt),
        in_specs=[pl.BlockSpec((1, TBLK, CONV_CH), lambda b, t: (b, t, ao)),
                  pl.BlockSpec((1, TBLK, CONV_CH), lambda b, t: (b, t, ao + 1)),
                  pl.BlockSpec((CONV_WIDTH, CONV_CH), lambda b, t: (0, 0)),
                  vec, vec, vec],
        out_specs=pl.BlockSpec((1, TBLK, CONV_CH), lambda b, t: (b, t, 0)),
        out_shape=jax.ShapeDtypeStruct((bsz, seq_len, CONV_CH), BF16),
        scratch_shapes=[pltpu.VMEM((CONV_HALO + TBLK, CONV_CH), F32),
                        pltpu.VMEM((TBLK, CONV_CH), F32)],
        compiler_params=_cparams(("parallel", "arbitrary"), 32),
        name="conformer_conv",
    )(p, p, dw_w, dw_b.reshape(1, CONV_CH), ln_g.reshape(1, CONV_CH), ln_b.reshape(1, CONV_CH))


MM_BM = 1376
MM_BN = 512

EV_Z_COL = 0
EV_XBC_COL = SSD_INNER
EV_DT_COL = SSD_INNER + SSD_CONV_CH
EV_A_COLS = -(-(EV_DT_COL + SSD_HEADS) // MM_BN) * MM_BN
EV_TAIL = EV_DT_COL + SSD_HEADS
EV_Q_COL = 0
EV_F_COL = 3 * FOX_WIDTH
EV_B_COLS = EV_F_COL + LANES
EV_B_BN = 896
OD_Q_COL = 0
OD_K_COL = GLA_KWIDTH
OD_V_COL = 2 * GLA_KWIDTH
OD_GLR_COL = 2 * GLA_KWIDTH + GLA_VWIDTH
OD_A_COLS = -(-(OD_GLR_COL + GLA_RANK) // MM_BN) * MM_BN
OD_TAIL = OD_GLR_COL + GLA_RANK
OD_R_COL = 0
OD_GLU_COL = GLA_VWIDTH
OD_B_COLS = GLA_VWIDTH + 2 * CONV_CH


def _out_proj_ln(a, c, w_out, layer, h, g, b, alpha, name):
    m = h.shape[0]
    mix = _matmul2(a.reshape(m, -1), c.reshape(m, -1), w_out, layer, bm=MM_BM, bn=256, vmem_mib=56, name=name)
    return _add_ln(h, mix, g, b, alpha=alpha, name=name + "_ln")


def _ffn_ln(h, hb, w_gate, w_up, w_down_b, layer, g, b, alpha, name):
    mid = _swiglu_up(hb, w_gate, w_up, layer, bm=MM_BM, bn=256, vmem_mib=48, name=name + "_up")
    down = _matmul(mid, w_down_b, layer, n_out=D_MODEL, bm=MM_BM // 2, bn=256, out_dtype=F32, vmem_mib=56,
                   name=name + "_down")
    return _add_ln(h, down, g, b, alpha=alpha, name=name + "_ln")


def kernel(x, meta_tokens, ev_w_in, ev_conv_w, ev_conv_b, ev_dt_bias, ev_a_log, ev_d_skip, ev_ssm_norm_w, ev_fgate_b, ev_w_out, od_w_in, od_w_gate_up, od_gate_b, od_gla_norm_w, od_dwconv_w, od_dwconv_b, od_conv_ln_g, od_conv_ln_b, od_w_out, ln_mix_g, ln_mix_b, ffn_w_gate, ffn_w_up, ffn_w_down, ln_ffn_g, ln_ffn_b):
    alpha = (2.0 * DEPTH) ** 0.25
    bsz = x.shape[0]
    meta = jnp.broadcast_to(meta_tokens[None].astype(x.dtype), (bsz, N_META, D_MODEL))
    h3 = jnp.concatenate([meta, x], axis=1)
    seq_len = h3.shape[1]
    h = h3.reshape(bsz * seq_len, D_MODEL)
    hb = h.astype(BF16)
    w_down_b = ffn_w_down.astype(BF16)
    for i in range(DEPTH):
        j = i // 2
        if i % 2 == 0:
            w_tail = jnp.pad(ev_w_in[j][:, EV_TAIL:], ((0, 0), (0, LANES - FOX_HEADS))).astype(BF16)[None]
            pa = _matmul(hb, ev_w_in, j, n_out=EV_A_COLS, bm=MM_BM, bn=MM_BN, out_dtype=F32, vmem_mib=52,
                         name="ev_in_a").reshape(bsz, seq_len, EV_A_COLS)
            pb = _matmul(hb, w_tail, 0, n_out=EV_B_COLS, bm=MM_BM // 2, bn=EV_B_BN, out_dtype=F32, vmem_mib=52,
                         name="ev_in_b").reshape(bsz, seq_len, EV_B_COLS)
            y = _ssd(pa, ev_conv_w[j], ev_conv_b[j], ev_dt_bias[j], ev_a_log[j], ev_d_skip[j], ev_ssm_norm_w[j],
                     xbc_col=EV_XBC_COL, z_col=EV_Z_COL, dt_col=EV_DT_COL)
            cum_t = _fox_cum(pb, ev_fgate_b[j], f_col=EV_F_COL)
            o = _fox(pb, cum_t, q_col=EV_Q_COL)
            h, hb = _out_proj_ln(y, o, ev_w_out, j, h, ln_mix_g[i], ln_mix_b[i], alpha, "ev_out")
        else:
            w_tail = od_w_in[j][:, OD_TAIL:].astype(BF16)[None]
            pa = _matmul(hb, od_w_in, j, n_out=OD_A_COLS, bm=MM_BM, bn=MM_BN, out_dtype=F32, vmem_mib=52,
                         name="od_in_a").reshape(bsz, seq_len, OD_A_COLS)
            pb = _matmul(hb, w_tail, 0, n_out=OD_B_COLS, bm=MM_BM, bn=MM_BN, out_dtype=F32, vmem_mib=52,
                         name="od_in_b").reshape(bsz, seq_len, OD_B_COLS)
            o = _gla(pa, pb, od_w_gate_up[j], od_gate_b[j], od_gla_norm_w[j],
                     q_col=OD_Q_COL, k_col=OD_K_COL, v_col=OD_V_COL, glr_col=OD_GLR_COL, r_col=OD_R_COL)
            u = _conformer_conv(pb, od_dwconv_w[j], od_dwconv_b[j], od_conv_ln_g[j], od_conv_ln_b[j],
                                glu_col=OD_GLU_COL)
            h, hb = _out_proj_ln(o, u, od_w_out, j, h, ln_mix_g[i], ln_mix_b[i], alpha, "od_out")
        h, hb = _ffn_ln(h, hb, ffn_w_gate, ffn_w_up, w_down_b, i, ln_ffn_g[i], ln_ffn_b[i], alpha, "ffn")
    return h.reshape(bsz, seq_len, D_MODEL)[:, N_META:]
```

```python
import functools

import jax
import jax.numpy as jnp
from jax import lax
from jax.experimental import pallas as pl
from jax.experimental.pallas import tpu as pltpu

F32 = jnp.float32
BF16 = jnp.bfloat16
HIGHEST = lax.Precision.HIGHEST

D_MODEL = 4096
DEPTH = 2
N_META = 16
SSD_HEADDIM = 64
SSD_INNER = D_MODEL
SSD_HEADS = SSD_INNER // SSD_HEADDIM
SSD_GROUPS = 8
SSD_HPG = SSD_HEADS // SSD_GROUPS
SSD_STATE = 128
SSD_CONV = 4
SSD_GROUP_W = SSD_HPG * SSD_HEADDIM
SSD_CONV_CH = SSD_INNER + 2 * SSD_GROUPS * SSD_STATE
FOX_HEADDIM = 128
FOX_WIDTH = D_MODEL // 2
FOX_HEADS = FOX_WIDTH // FOX_HEADDIM
NEG_INF = -1e30
GLA_HEADS = D_MODEL // 512
GLA_DK = 256
GLA_DV = 512
GLA_KWIDTH = GLA_HEADS * GLA_DK
GLA_VWIDTH = GLA_HEADS * GLA_DV
GLA_RANK = 16
GLA_TAU = 16.0
GLA_CHUNK = 64
CONV_CH = D_MODEL // 2
CONV_WIDTH = 31
D_FF = -(-8 * D_MODEL // (3 * 256)) * 256
LN_EPS = 1e-5

LANES = 128
TBLK = 128
CONV_HALO = 32
MIB = 1024 * 1024


def _cparams(semantics, vmem_mib):
    return pltpu.CompilerParams(dimension_semantics=semantics, vmem_limit_bytes=vmem_mib * MIB)


def _tile(dim, target):
    return dim if dim < target else target


def _sigmoid(x):
    return 1.0 / (1.0 + jnp.exp(-x))


def _silu(x):
    return x * _sigmoid(x)


def _softplus(x):
    return jnp.maximum(x, 0.0) + jnp.log1p(jnp.exp(-jnp.abs(x)))


def _log_sigmoid(x):
    return jnp.minimum(x, 0.0) - jnp.log1p(jnp.exp(-jnp.abs(x)))


def _split_dot(v, onehot, parts):
    out = None
    rem = v
    for _ in range(parts):
        piece = rem.astype(BF16)
        term = jnp.dot(piece, onehot, preferred_element_type=F32)
        out = term if out is None else out + term
        rem = rem - piece.astype(F32)
    return out


def _tril(n):
    r = lax.broadcasted_iota(jnp.int32, (n, n), 0)
    c = lax.broadcasted_iota(jnp.int32, (n, n), 1)
    return c <= r


def _mm_kernel(x_ref, w_ref, o_ref):
    o_ref[...] = jnp.dot(x_ref[...], w_ref[...].astype(BF16), preferred_element_type=F32).astype(o_ref.dtype)


def _mm_nt_kernel(x_ref, wt_ref, o_ref):
    o_ref[...] = lax.dot_general(x_ref[...], wt_ref[...].astype(BF16), (((1,), (1,)), ((), ())),
                                 preferred_element_type=F32).astype(o_ref.dtype)


def _mm_res_kernel(x_ref, w_ref, h_ref, o_ref, *, alpha):
    acc = jnp.dot(x_ref[...], w_ref[...].astype(BF16), preferred_element_type=F32)
    o_ref[...] = alpha * h_ref[...] + acc


def _matmul(x, w, layer, *, n_out, bm, bn, out_dtype, vmem_mib, name, res=None, alpha=None, w_is_nk=False):
    m, k = x.shape
    bm = _tile(m, bm)
    tile = pl.BlockSpec((bm, bn), lambda i, j: (i, j))
    if w_is_nk:
        body, wspec = _mm_nt_kernel, pl.BlockSpec((None, bn, k), lambda i, j: (layer, j, 0))
    else:
        body, wspec = _mm_kernel, pl.BlockSpec((None, k, bn), lambda i, j: (layer, 0, j))
    in_specs = [pl.BlockSpec((bm, k), lambda i, j: (i, 0)), wspec]
    args = (x, w)
    if res is not None:
        assert not w_is_nk
        body, args = functools.partial(_mm_res_kernel, alpha=alpha), (x, w, res)
        in_specs.append(tile)
    return pl.pallas_call(
        body,
        grid=(pl.cdiv(m, bm), n_out // bn),
        in_specs=in_specs,
        out_specs=tile,
        out_shape=jax.ShapeDtypeStruct((m, n_out), out_dtype),
        compiler_params=_cparams(("parallel", "arbitrary"), vmem_mib),
        name=name,
    )(*args)


def _mm2_kernel(a_ref, c_ref, w1_ref, w2_ref, h_ref, o_ref, *, alpha):
    acc = jnp.dot(a_ref[...], w1_ref[...].astype(BF16), preferred_element_type=F32)
    acc = acc + jnp.dot(c_ref[...], w2_ref[...].astype(BF16), preferred_element_type=F32)
    o_ref[...] = alpha * h_ref[...] + acc


def _matmul2(a, c, w, layer, res, *, alpha, bm, bn, vmem_mib, name):
    m, k1 = a.shape
    k2 = c.shape[1]
    n = w.shape[2]
    bm = _tile(m, bm)
    tile = pl.BlockSpec((bm, bn), lambda i, j: (i, j))
    return pl.pallas_call(
        functools.partial(_mm2_kernel, alpha=alpha),
        grid=(pl.cdiv(m, bm), n // bn),
        in_specs=[pl.BlockSpec((bm, k1), lambda i, j: (i, 0)),
                  pl.BlockSpec((bm, k2), lambda i, j: (i, 0)),
                  pl.BlockSpec((None, k1, bn), lambda i, j: (layer, 0, j)),
                  pl.BlockSpec((None, k2, bn), lambda i, j: (layer, k1 // k2, j)),
                  tile],
        out_specs=tile,
        out_shape=jax.ShapeDtypeStruct((m, n), F32),
        compiler_params=_cparams(("parallel", "arbitrary"), vmem_mib),
        name=name,
    )(a, c, w, w, res)


def _swiglu_up_kernel(x_ref, wg_ref, wu_ref, o_ref):
    x = x_ref[...]
    g = jnp.dot(x, wg_ref[...].astype(BF16), preferred_element_type=F32)
    u = jnp.dot(x, wu_ref[...].astype(BF16), preferred_element_type=F32)
    o_ref[...] = (_silu(g) * u).astype(o_ref.dtype)


def _swiglu_up(x, wg, wu, layer, *, bm, bn, vmem_mib, name):
    m, k = x.shape
    n = wg.shape[2]
    bm = _tile(m, bm)
    wspec = pl.BlockSpec((None, k, bn), lambda i, j: (layer, 0, j))
    return pl.pallas_call(
        _swiglu_up_kernel,
        grid=(pl.cdiv(m, bm), n // bn),
        in_specs=[pl.BlockSpec((bm, k), lambda i, j: (i, 0)), wspec, wspec],
        out_specs=pl.BlockSpec((bm, bn), lambda i, j: (i, j)),
        out_shape=jax.ShapeDtypeStruct((m, n), BF16),
        compiler_params=_cparams(("parallel", "arbitrary"), vmem_mib),
        name=name,
    )(x, wg, wu)


def _layer_norm(y, g, b):
    yc = y - jnp.mean(y, axis=-1, keepdims=True)
    var = jnp.mean(yc * yc, axis=-1, keepdims=True)
    return yc * lax.rsqrt(var + LN_EPS) * g + b


def _ln_kernel(y_ref, g_ref, b_ref, of_ref, ob_ref):
    out = _layer_norm(y_ref[...], g_ref[...], b_ref[...])
    of_ref[...] = out
    ob_ref[...] = out.astype(BF16)


def _ln(y, g, b, *, name):
    m, d = y.shape
    bm = _tile(m, TBLK)
    row = pl.BlockSpec((bm, d), lambda i: (i, 0))
    vec = pl.BlockSpec((1, d), lambda i: (0, 0))
    return pl.pallas_call(
        _ln_kernel,
        grid=(pl.cdiv(m, bm),),
        in_specs=[row, vec, vec],
        out_specs=[row, row],
        out_shape=[jax.ShapeDtypeStruct((m, d), F32), jax.ShapeDtypeStruct((m, d), BF16)],
        compiler_params=_cparams(("parallel",), 32),
        name=name,
    )(y, g.reshape(1, d), b.reshape(1, d))


def _ln_drop_kernel(ya_ref, yb_ref, g_ref, b_ref, o_ref, *, drop):
    y = jnp.concatenate([ya_ref[0, drop:, :], yb_ref[0]], axis=0)
    o_ref[0] = _layer_norm(y, g_ref[...], b_ref[...])


def _ln_drop(y, g, b, *, drop, name):
    bsz, seq_len, d = y.shape
    nt = (seq_len - drop) // TBLK
    assert (seq_len - drop) % TBLK == 0 and TBLK % drop == 0 and drop % 8 == 0
    vec = pl.BlockSpec((1, d), lambda bi, t: (0, 0))
    return pl.pallas_call(
        functools.partial(_ln_drop_kernel, drop=drop),
        grid=(bsz, nt),
        in_specs=[pl.BlockSpec((1, TBLK, d), lambda bi, t: (bi, t, 0)),
                  pl.BlockSpec((1, drop, d), lambda bi, t: (bi, (t + 1) * (TBLK // drop), 0)),
                  vec, vec],
        out_specs=pl.BlockSpec((1, TBLK, d), lambda bi, t: (bi, t, 0)),
        out_shape=jax.ShapeDtypeStruct((bsz, seq_len - drop, d), F32),
        compiler_params=_cparams(("parallel", "parallel"), 32),
        name=name,
    )(y, y, g.reshape(1, d), b.reshape(1, d))


def _ssd_kernel(xs_ref, bc_ref, z_ref, dt_ref, e_ref, cwx_ref, cwb_ref, cbx_ref, cbb_ref,
                dtb_ref, alog_ref, dexp_ref, nw_ref, y_ref,
                s_ref, extx_ref, extb_ref, actx_ref, actb_ref, *, seq_len):
    c = pl.program_id(1)

    @pl.when(c == 0)
    def _():
        s_ref[...] = jnp.zeros_like(s_ref)
        extx_ref[0:8, :] = jnp.zeros((8, SSD_INNER), F32)
        extb_ref[0:8, :] = jnp.zeros((8, SSD_CONV_CH - SSD_INNER), F32)

    valid = seq_len - c * TBLK
    rmask = lax.broadcasted_iota(jnp.int32, (TBLK, 1), 0) < valid

    def conv_silu(raw_ref, ext_ref, w_ref, bias_ref, act_ref):
        raw = jnp.where(rmask, raw_ref[0], 0.0)
        ext_ref[8:8 + TBLK, :] = raw
        acc = bias_ref[...] + w_ref[SSD_CONV - 1:SSD_CONV, :] * raw
        for k in range(SSD_CONV - 1):
            lo = 8 - (SSD_CONV - 1) + k
            acc = acc + w_ref[k:k + 1, :] * ext_ref[lo:lo + TBLK, :]
        ext_ref[0:8, :] = raw[TBLK - 8:TBLK, :]
        act_ref[...] = _silu(acc)

    conv_silu(xs_ref, extx_ref, cwx_ref, cbx_ref, actx_ref)
    conv_silu(bc_ref, extb_ref, cwb_ref, cbb_ref, actb_ref)

    dt_all = jnp.where(rmask, _softplus(dt_ref[0] + dtb_ref[...]), 0.0)
    la_all = dt_all * (-jnp.exp(alog_ref[...]))
    tri = _tril(TBLK)
    a_cs = jnp.dot(tri.astype(F32), la_all, precision=HIGHEST, preferred_element_type=F32)
    a_cs_t = a_cs.T
    e_in = jnp.exp(a_cs)
    e_out = jnp.exp(a_cs[TBLK - 1:TBLK, :] - a_cs)
    stack = jnp.concatenate([dt_all, e_in, e_out], axis=0)
    stack_hi = stack.astype(BF16)
    stack_lo = (stack - stack_hi.astype(F32)).astype(BF16)
    lane = lax.broadcasted_iota(jnp.int32, (TBLK, LANES), 1)
    gw, st = SSD_GROUP_W, SSD_STATE
    for g in range(SSD_GROUPS):
        eg = e_ref[:, g * gw:(g + 1) * gw]
        expanded = (jnp.dot(stack_hi, eg, preferred_element_type=F32)
                    + jnp.dot(stack_lo, eg, preferred_element_type=F32))
        dt_e = expanded[0:TBLK]
        ein_e = expanded[TBLK:2 * TBLK]
        eout_e = expanded[2 * TBLK:3 * TBLK]
        xs = actx_ref[:, g * gw:(g + 1) * gw]
        bb = actb_ref[:, g * st:(g + 1) * st].astype(BF16)
        cc = actb_ref[:, (SSD_GROUPS + g) * st:(SSD_GROUPS + g + 1) * st].astype(BF16)
        xdt = xs * dt_e
        state = s_ref[g]
        y = jnp.dot(cc, state.astype(BF16), preferred_element_type=F32) * ein_e
        cb = lax.dot_general(cc, bb, (((1,), (1,)), ((), ())), preferred_element_type=F32)
        cols = []
        for m in range(gw // LANES):
            xcol = xdt[:, m * LANES:(m + 1) * LANES]
            ycol = None
            for half in range(LANES // SSD_HEADDIM):
                hh = g * SSD_HPG + m * (LANES // SSD_HEADDIM) + half
                colb = jnp.broadcast_to(a_cs[:, hh:hh + 1], (TBLK, TBLK))
                rowb = jnp.broadcast_to(a_cs_t[hh:hh + 1, :], (TBLK, TBLK))
                lmat = jnp.where(tri, jnp.exp(colb - rowb), 0.0) * cb
                in_half = (lane >= half * SSD_HEADDIM) & (lane < (half + 1) * SSD_HEADDIM)
                xh = jnp.where(in_half, xcol, 0.0).astype(BF16)
                term = jnp.dot(lmat.astype(BF16), xh, preferred_element_type=F32)
                ycol = term if ycol is None else ycol + term
            cols.append(ycol)
        y = y + jnp.concatenate(cols, axis=1) + xs * dexp_ref[:, g * gw:(g + 1) * gw]

        xd = (xdt * eout_e).astype(BF16)
        s_ref[g] = state * ein_e[TBLK - 1:TBLK, :] + lax.dot_general(
            bb, xd, (((0,), (0,)), ((), ())), preferred_element_type=F32)

        yz = y * _silu(z_ref[0, :, g * gw:(g + 1) * gw])
        ms = jnp.mean(yz * yz, axis=-1, keepdims=True)
        y_ref[0, :, g * gw:(g + 1) * gw] = (
            yz * lax.rsqrt(ms + LN_EPS) * nw_ref[:, g * gw:(g + 1) * gw]).astype(y_ref.dtype)


def _ssd(p, conv_w, conv_b, dt_bias, a_log, d_skip, norm_w, *, xbc_col, z_col, dt_col):
    bsz, seq_len, _ = p.shape
    nc = pl.cdiv(seq_len, TBLK)
    bcw = SSD_CONV_CH - SSD_INNER
    xoff, boff, zoff, doff = xbc_col // SSD_INNER, (xbc_col + SSD_INNER) // bcw, z_col // SSD_INNER, dt_col // LANES
    expand = ((jnp.arange(SSD_INNER)[None, :] // SSD_HEADDIM) == jnp.arange(LANES)[:, None]).astype(BF16)
    pad = LANES - SSD_HEADS
    dtb = jnp.pad(dt_bias, (0, pad)).reshape(1, LANES)
    alog = jnp.pad(a_log, (0, pad)).reshape(1, LANES)
    dexp = jnp.repeat(d_skip, SSD_HEADDIM).reshape(1, SSD_INNER)
    cb2 = conv_b.reshape(1, SSD_CONV_CH)
    const = lambda shape, col=0: pl.BlockSpec(shape, lambda b, c: (0, col))
    return pl.pallas_call(
        functools.partial(_ssd_kernel, seq_len=seq_len),
        grid=(bsz, nc),
        in_specs=[pl.BlockSpec((1, TBLK, SSD_INNER), lambda b, c: (b, c, xoff)),
                  pl.BlockSpec((1, TBLK, bcw), lambda b, c: (b, c, boff)),
                  pl.BlockSpec((1, TBLK, SSD_INNER), lambda b, c: (b, c, zoff)),
                  pl.BlockSpec((1, TBLK, LANES), lambda b, c: (b, c, doff)),
                  const((LANES, SSD_INNER)),
                  const((SSD_CONV, SSD_INNER)), const((SSD_CONV, bcw), SSD_INNER // bcw),
                  const((1, SSD_INNER)), const((1, bcw), SSD_INNER // bcw),
                  const((1, LANES)), const((1, LANES)),
                  const((1, SSD_INNER)), const((1, SSD_INNER))],
        out_specs=pl.BlockSpec((1, TBLK, SSD_INNER), lambda b, c: (b, c, 0)),
        out_shape=jax.ShapeDtypeStruct((bsz, seq_len, SSD_INNER), BF16),
        scratch_shapes=[pltpu.VMEM((SSD_GROUPS, SSD_STATE, SSD_GROUP_W), F32),
                        pltpu.VMEM((8 + TBLK, SSD_INNER), F32),
                        pltpu.VMEM((8 + TBLK, bcw), F32),
                        pltpu.VMEM((TBLK, SSD_INNER), F32),
                        pltpu.VMEM((TBLK, bcw), F32)],
        compiler_params=_cparams(("parallel", "arbitrary"), 48),
        name="ssd_scan",
    )(p, p, p, p, expand, conv_w, conv_w, cb2, cb2, dtb, alog, dexp, norm_w.reshape(1, SSD_INNER))


FOX_QTILE = 512


def _fox_cum_kernel(p_ref, b_ref, o_ref, *, seq_len):
    nfull, tail = seq_len // TBLK, seq_len % TBLK
    r = lax.broadcasted_iota(jnp.int32, (TBLK, TBLK), 0)
    c = lax.broadcasted_iota(jnp.int32, (TBLK, TBLK), 1)
    upper = (r <= c).astype(F32)
    carry = jnp.zeros((LANES, 1), F32)
    for blk in range(nfull + (1 if tail else 0)):
        rows = TBLK if blk < nfull else tail
        lf = _log_sigmoid(p_ref[0, blk * TBLK:blk * TBLK + rows, :] + b_ref[...])
        if rows < TBLK:
            lf = jnp.concatenate([lf, jnp.zeros((TBLK - rows, LANES), F32)], axis=0)
        cs = jnp.dot(lf.T, upper, precision=HIGHEST, preferred_element_type=F32) + carry
        o_ref[0, :, blk * TBLK:(blk + 1) * TBLK] = cs
        carry = cs[:, TBLK - 1:TBLK]


def _fox_cum(p, fgate_b, *, f_col):
    bsz, seq_len, _ = p.shape
    lp = pl.cdiv(seq_len, TBLK) * TBLK
    foff = f_col // LANES
    bias = jnp.pad(fgate_b, (0, LANES - FOX_HEADS)).reshape(1, LANES)
    return pl.pallas_call(
        functools.partial(_fox_cum_kernel, seq_len=seq_len),
        grid=(bsz,),
        in_specs=[pl.BlockSpec((1, seq_len, LANES), lambda b: (b, 0, foff)),
                  pl.BlockSpec((1, LANES), lambda b: (0, 0))],
        out_specs=pl.BlockSpec((1, LANES, lp), lambda b: (b, 0, 0)),
        out_shape=jax.ShapeDtypeStruct((bsz, LANES, lp), F32),
        compiler_params=_cparams(("parallel",), 32),
        name="fox_cum",
    )(p, bias)


def _fox_tiles(seq_len):
    aligned = (seq_len // TBLK) * TBLK
    tiles, start = [], 0
    while start < aligned:
        rows = min(FOX_QTILE, aligned - start)
        tiles.append((start, rows, start + rows))
        start += rows
    if seq_len > aligned:
        tiles.append((aligned, seq_len - aligned, aligned + TBLK))
    return tiles


def _fox_kernel(q_ref, k_ref, v_ref, ck_ref, o_ref, kt_ref, vb_ref, *, seq_len):
    for lo in range(0, seq_len, TBLK):
        rows = min(TBLK, seq_len - lo)
        kb = k_ref[0, lo:lo + rows, :]
        vb = v_ref[0, lo:lo + rows, :]
        if rows < TBLK:
            zpad = jnp.zeros((TBLK - rows, FOX_HEADDIM), F32)
            kb = jnp.concatenate([kb, zpad], axis=0)
            vb = jnp.concatenate([vb, zpad], axis=0)
        kt_ref[:, lo:lo + TBLK] = kb.T.astype(BF16)
        vb_ref[lo:lo + TBLK, :] = vb.astype(BF16)

    for start, rows, lk in _fox_tiles(seq_len):
        q = (q_ref[0, start:start + rows, :] * (FOX_HEADDIM ** -0.5)).astype(BF16)
        dw = lk - start
        r = lax.broadcasted_iota(jnp.int32, (rows, dw), 0)
        c = lax.broadcasted_iota(jnp.int32, (rows, dw), 1)
        sd = jnp.dot(q, kt_ref[:, start:lk], preferred_element_type=F32) - ck_ref[0, 0, :, start:lk]
        sd = jnp.where(c <= r, sd, NEG_INF)
        m = jnp.max(sd, axis=-1, keepdims=True)
        if start:
            so = jnp.dot(q, kt_ref[:, 0:start], preferred_element_type=F32) - ck_ref[0, 0, :, 0:start]
            m = jnp.maximum(m, jnp.max(so, axis=-1, keepdims=True))
        pd = jnp.exp(sd - m)
        l = jnp.sum(pd, axis=-1, keepdims=True)
        o = jnp.dot(pd.astype(BF16), vb_ref[start:lk, :], preferred_element_type=F32)
        if start:
            po = jnp.exp(so - m)
            l = l + jnp.sum(po, axis=-1, keepdims=True)
            o = o + jnp.dot(po.astype(BF16), vb_ref[0:start, :], preferred_element_type=F32)
        o_ref[0, start:start + rows, :] = (o / l).astype(o_ref.dtype)


def _fox(p, cum_t, *, q_col):
    bsz, seq_len, _ = p.shape
    lp = cum_t.shape[2]
    qo = q_col // FOX_HEADDIM
    ko = qo + FOX_HEADS
    vo = ko + FOX_HEADS
    ck = cum_t.reshape(bsz, LANES, 1, lp)
    return pl.pallas_call(
        functools.partial(_fox_kernel, seq_len=seq_len),
        grid=(bsz, FOX_HEADS),
        in_specs=[pl.BlockSpec((1, seq_len, FOX_HEADDIM), lambda b, h: (b, 0, qo + h)),
                  pl.BlockSpec((1, seq_len, FOX_HEADDIM), lambda b, h: (b, 0, ko + h)),
                  pl.BlockSpec((1, seq_len, FOX_HEADDIM), lambda b, h: (b, 0, vo + h)),
                  pl.BlockSpec((1, 1, 1, lp), lambda b, h: (b, h, 0, 0))],
        out_specs=pl.BlockSpec((1, seq_len, FOX_HEADDIM), lambda b, h: (b, 0, h)),
        out_shape=jax.ShapeDtypeStruct((bsz, seq_len, FOX_WIDTH), BF16),
        scratch_shapes=[pltpu.VMEM((FOX_HEADDIM, lp), BF16),
                        pltpu.VMEM((lp, FOX_HEADDIM), BF16)],
        compiler_params=_cparams(("parallel", "parallel"), 48),
        name="fox_attn",
    )(p, p, p, ck)


GLA_TBLK = 256
GLA_HPS = 4


def _gla_kernel(q_ref, k_ref, v_ref, r_ref, glr_ref, wgu_ref, gb_ref, nw_ref, o_ref, st_ref, *, seq_len):
    t = pl.program_id(2)

    @pl.when(t == 0)
    def _():
        st_ref[...] = jnp.zeros_like(st_ref)

    valid = seq_len - t * GLA_TBLK
    rmask = lax.broadcasted_iota(jnp.int32, (GLA_TBLK, 1), 0) < valid
    glr = glr_ref[0].astype(BF16)
    tri = _tril(GLA_CHUNK)
    tri_f = tri.astype(F32)
    mid = GLA_CHUNK // 2
    for hd in range(GLA_HPS):
        ks = slice(hd * GLA_DK, (hd + 1) * GLA_DK)
        vs = slice(hd * GLA_DV, (hd + 1) * GLA_DV)
        gpre = jnp.dot(glr, wgu_ref[:, ks], preferred_element_type=F32) + gb_ref[:, ks]
        g_all = jnp.where(rmask, _log_sigmoid(gpre) * (1.0 / GLA_TAU), 0.0)
        q_all = jnp.where(rmask, q_ref[0, :, ks], 0.0) * (GLA_DK ** -0.5)
        k_all = jnp.where(rmask, k_ref[0, :, ks], 0.0)
        v_all = jnp.where(rmask, v_ref[0, :, vs], 0.0)
        outs = []
        for ci in range(GLA_TBLK // GLA_CHUNK):
            sl = slice(ci * GLA_CHUNK, (ci + 1) * GLA_CHUNK)
            qc, kc, vc, gc = q_all[sl], k_all[sl], v_all[sl], g_all[sl]
            vcb = vc.astype(BF16)
            bcum = jnp.dot(tri_f, gc, precision=HIGHEST, preferred_element_type=F32)
            bref = bcum[mid:mid + 1, :]
            btot = bcum[GLA_CHUNK - 1:GLA_CHUNK, :]
            qe = (qc * jnp.exp(bcum - bref)).astype(BF16)
            ke = (kc * jnp.exp(bref - bcum)).astype(BF16)
            scores = lax.dot_general(qe, ke, (((1,), (1,)), ((), ())), preferred_element_type=F32)
            scores = jnp.where(tri, scores, 0.0).astype(BF16)
            state_t = st_ref[hd]
            qd = (qc * jnp.exp(bcum)).astype(BF16)
            o = jnp.dot(scores, vcb, preferred_element_type=F32) + lax.dot_general(
                qd, state_t.astype(BF16), (((1,), (1,)), ((), ())), preferred_element_type=F32)
            kd = (kc * jnp.exp(btot - bcum)).astype(BF16)
            st_ref[hd] = state_t * jnp.exp(btot) + lax.dot_general(
                vcb, kd, (((0,), (0,)), ((), ())), preferred_element_type=F32)
            outs.append(o)
        o = jnp.concatenate(outs, axis=0)
        ms = jnp.mean(o * o, axis=-1, keepdims=True)
        o = o * lax.rsqrt(ms + LN_EPS) * nw_ref[...]
        o_ref[0, :, vs] = (o * _silu(r_ref[0, :, vs])).astype(o_ref.dtype)


def _gla(p, pr, w_gate_up, gate_b, norm_w, *, q_col, k_col, v_col, glr_col, r_col):
    bsz, seq_len, _ = p.shape
    nt = pl.cdiv(seq_len, GLA_TBLK)
    kw, vw = GLA_HPS * GLA_DK, GLA_HPS * GLA_DV
    qo, ko, vo, ro, go = q_col // kw, k_col // kw, v_col // vw, r_col // vw, glr_col // LANES
    wgu = jnp.pad(w_gate_up, ((0, LANES - GLA_RANK), (0, 0))).astype(BF16)
    return pl.pallas_call(
        functools.partial(_gla_kernel, seq_len=seq_len),
        grid=(bsz, GLA_HEADS // GLA_HPS, nt),
        in_specs=[pl.BlockSpec((1, GLA_TBLK, kw), lambda b, h, t: (b, t, qo + h)),
                  pl.BlockSpec((1, GLA_TBLK, kw), lambda b, h, t: (b, t, ko + h)),
                  pl.BlockSpec((1, GLA_TBLK, vw), lambda b, h, t: (b, t, vo + h)),
                  pl.BlockSpec((1, GLA_TBLK, vw), lambda b, h, t: (b, t, ro + h)),
                  pl.BlockSpec((1, GLA_TBLK, LANES), lambda b, h, t: (b, t, go)),
                  pl.BlockSpec((LANES, kw), lambda b, h, t: (0, h)),
                  pl.BlockSpec((1, kw), lambda b, h, t: (0, h)),
                  pl.BlockSpec((1, GLA_DV), lambda b, h, t: (0, 0))],
        out_specs=pl.BlockSpec((1, GLA_TBLK, vw), lambda b, h, t: (b, t, h)),
        out_shape=jax.ShapeDtypeStruct((bsz, seq_len, GLA_VWIDTH), BF16),
        scratch_shapes=[pltpu.VMEM((GLA_HPS, GLA_DV, GLA_DK), F32)],
        compiler_params=_cparams(("parallel", "parallel", "arbitrary"), 40),
        name="gla_scan",
    )(p, p, p, pr, p, wgu, gate_b.reshape(1, GLA_KWIDTH), norm_w.reshape(1, GLA_DV))


CONV_CBLK = 256


def _conf_kernel(ua_ref, ug_ref, w_ref, b_ref, g_ref, be_ref, o_ref, ext_ref, acc_ref, sh_ref, *, seq_len):
    t = pl.program_id(1)

    @pl.when(t == 0)
    def _():
        ext_ref[0:CONV_HALO, :] = jnp.zeros((CONV_HALO, CONV_CH), F32)

    valid = seq_len - t * TBLK
    rmask = lax.broadcasted_iota(jnp.int32, (TBLK, 1), 0) < valid
    glu = jnp.where(rmask, ua_ref[0] * _sigmoid(ug_ref[0]), 0.0)
    ext_ref[CONV_HALO:CONV_HALO + TBLK, :] = glu
    base = CONV_HALO - (CONV_WIDTH - 1)
    for cb in range(CONV_CH // CONV_CBLK):
        cs = slice(cb * CONV_CBLK, (cb + 1) * CONV_CBLK)
        acc = jnp.broadcast_to(b_ref[:, cs], (TBLK, CONV_CBLK))
        for r in range(8):
            taps = [k for k in range(CONV_WIDTH) if (base + k) % 8 == r]
            span = 8 * max((base + k) // 8 for k in taps) + TBLK
            sh_ref[0:span, :] = ext_ref[r:r + span, cs]
            for k in taps:
                lo = 8 * ((base + k) // 8)
                acc = acc + w_ref[k:k + 1, cs] * sh_ref[lo:lo + TBLK, :]
        acc_ref[:, cs] = acc
    ext_ref[0:CONV_HALO, :] = ext_ref[TBLK:TBLK + CONV_HALO, :]
    u = acc_ref[...]
    uc = u - jnp.mean(u, axis=-1, keepdims=True)
    var = jnp.mean(uc * uc, axis=-1, keepdims=True)
    o_ref[0] = _silu(uc * lax.rsqrt(var + LN_EPS) * g_ref[...] + be_ref[...]).astype(o_ref.dtype)


def _conformer_conv(p, dw_w, dw_b, ln_g, ln_b, *, glu_col):
    bsz, seq_len, _ = p.shape
    nt = pl.cdiv(seq_len, TBLK)
    ao = glu_col // CONV_CH
    vec = pl.BlockSpec((1, CONV_CH), lambda b, t: (0, 0))
    return pl.pallas_call(
        functools.partial(_conf_kernel, seq_len=seq_len),
        grid=(bsz, nt),
        in_specs=[pl.BlockSpec((1, TBLK, CONV_CH), lambda b, t: (b, t, ao)),
                  pl.BlockSpec((1, TBLK, CONV_CH), lambda b, t: (b, t, ao + 1)),
                  pl.BlockSpec((CONV_WIDTH, CONV_CH), lambda b, t: (0, 0)),
                  vec, vec, vec],
        out_specs=pl.BlockSpec((1, TBLK, CONV_CH), lambda b, t: (b, t, 0)),
        out_shape=jax.ShapeDtypeStruct((bsz, seq_len, CONV_CH), BF16),
        scratch_shapes=[pltpu.VMEM((CONV_HALO + TBLK, CONV_CH), F32),
                        pltpu.VMEM((TBLK, CONV_CH), F32),
                        pltpu.VMEM((CONV_HALO + TBLK, CONV_CBLK), F32)],
        compiler_params=_cparams(("parallel", "arbitrary"), 32),
        name="conformer_conv",
    )(p, p, dw_w, dw_b.reshape(1, CONV_CH), ln_g.reshape(1, CONV_CH), ln_b.reshape(1, CONV_CH))


MM_BM = 1376
MM_BN = 512

EV_Z_COL = 0
EV_XBC_COL = SSD_INNER
EV_DT_COL = SSD_INNER + SSD_CONV_CH
EV_A_COLS = -(-(EV_DT_COL + SSD_HEADS) // MM_BN) * MM_BN
EV_TAIL = EV_DT_COL + SSD_HEADS
EV_Q_COL = 0
EV_F_COL = 3 * FOX_WIDTH
EV_B_COLS = EV_F_COL + LANES
EV_B_BN = 896
OD_Q_COL = 0
OD_K_COL = GLA_KWIDTH
OD_V_COL = 2 * GLA_KWIDTH
OD_GLR_COL = 2 * GLA_KWIDTH + GLA_VWIDTH
OD_A_COLS = -(-(OD_GLR_COL + GLA_RANK) // MM_BN) * MM_BN
OD_TAIL = OD_GLR_COL + GLA_RANK
OD_R_COL = 0
OD_GLU_COL = GLA_VWIDTH
OD_B_COLS = GLA_VWIDTH + 2 * CONV_CH


def _out_proj_ln(a, c, w_out, layer, h, g, b, alpha, name):
    m = h.shape[0]
    y = _matmul2(a.reshape(m, -1), c.reshape(m, -1), w_out, layer, h, alpha=alpha, bm=MM_BM, bn=256, vmem_mib=58,
                 name=name)
    return _ln(y, g, b, name=name + "_ln")


def _ffn(h, hb, w_gate, w_up, w_down_b, layer, alpha, name):
    mid = _swiglu_up(hb, w_gate, w_up, layer, bm=MM_BM, bn=256, vmem_mib=48, name=name + "_up")
    return _matmul(mid, w_down_b, layer, n_out=D_MODEL, bm=MM_BM // 2, bn=256, out_dtype=F32, vmem_mib=56,
                   name=name + "_down", res=h, alpha=alpha)


def kernel(x, meta_tokens, ev_w_in, ev_conv_w, ev_conv_b, ev_dt_bias, ev_a_log, ev_d_skip, ev_ssm_norm_w, ev_fgate_b, ev_w_out, od_w_in, od_w_gate_up, od_gate_b, od_gla_norm_w, od_dwconv_w, od_dwconv_b, od_conv_ln_g, od_conv_ln_b, od_w_out, ln_mix_g, ln_mix_b, ffn_w_gate, ffn_w_up, ffn_w_down, ln_ffn_g, ln_ffn_b):
    alpha = (2.0 * DEPTH) ** 0.25
    bsz = x.shape[0]
    meta = jnp.broadcast_to(meta_tokens[None].astype(x.dtype), (bsz, N_META, D_MODEL))
    h3 = jnp.concatenate([meta, x], axis=1)
    seq_len = h3.shape[1]
    h = h3.reshape(bsz * seq_len, D_MODEL)
    hb = h.astype(BF16)
    w_down_b = ffn_w_down.astype(BF16)
    for i in range(DEPTH):
        j = i // 2
        if i % 2 == 0:
            w_nk = jnp.swapaxes(ev_w_in, 1, 2)
            w_tail = jnp.pad(w_nk[j, EV_TAIL:], ((0, LANES - FOX_HEADS), (0, 0))).astype(BF16)[None]
            pa = _matmul(hb, w_nk, j, n_out=EV_A_COLS, bm=MM_BM, bn=MM_BN, out_dtype=F32, vmem_mib=52,
                         name="ev_in_a", w_is_nk=True).reshape(bsz, seq_len, EV_A_COLS)
            pb = _matmul(hb, w_tail, 0, n_out=EV_B_COLS, bm=MM_BM // 2, bn=EV_B_BN, out_dtype=F32, vmem_mib=52,
                         name="ev_in_b", w_is_nk=True).reshape(bsz, seq_len, EV_B_COLS)
            y = _ssd(pa, ev_conv_w[j], ev_conv_b[j], ev_dt_bias[j], ev_a_log[j], ev_d_skip[j], ev_ssm_norm_w[j],
                     xbc_col=EV_XBC_COL, z_col=EV_Z_COL, dt_col=EV_DT_COL)
            cum_t = _fox_cum(pb, ev_fgate_b[j], f_col=EV_F_COL)
            o = _fox(pb, cum_t, q_col=EV_Q_COL)
            h, hb = _out_proj_ln(y, o, ev_w_out, j, h, ln_mix_g[i], ln_mix_b[i], alpha, "ev_out")
        else:
            w_nk = jnp.swapaxes(od_w_in, 1, 2)
            w_tail = w_nk[j, OD_TAIL:].astype(BF16)[None]
            pa = _matmul(hb, w_nk, j, n_out=OD_A_COLS, bm=MM_BM, bn=MM_BN, out_dtype=F32, vmem_mib=52,
                         name="od_in_a", w_is_nk=True).reshape(bsz, seq_len, OD_A_COLS)
            pb = _matmul(hb, w_tail, 0, n_out=OD_B_COLS, bm=MM_BM, bn=MM_BN, out_dtype=F32, vmem_mib=52,
                         name="od_in_b", w_is_nk=True).reshape(bsz, seq_len, OD_B_COLS)
            o = _gla(pa, pb, od_w_gate_up[j], od_gate_b[j], od_gla_norm_w[j],
                     q_col=OD_Q_COL, k_col=OD_K_COL, v_col=OD_V_COL, glr_col=OD_GLR_COL, r_col=OD_R_COL)
            u = _conformer_conv(pb, od_dwconv_w[j], od_dwconv_b[j], od_conv_ln_g[j], od_conv_ln_b[j],
                                glu_col=OD_GLU_COL)
            h, hb = _out_proj_ln(o, u, od_w_out, j, h, ln_mix_g[i], ln_mix_b[i], alpha, "od_out")
        y = _ffn(h, hb, ffn_w_gate, ffn_w_up, w_down_b, i, alpha, "ffn")
        if i + 1 < DEPTH:
            h, hb = _ln(y, ln_ffn_g[i], ln_ffn_b[i], name="ffn_ln")
    return _ln_drop(y.reshape(bsz, seq_len, D_MODEL), ln_ffn_g[DEPTH - 1], ln_ffn_b[DEPTH - 1], drop=N_META,
                    name="final_ln")
```

```python
import functools

import jax
import jax.numpy as jnp
from jax import lax
from jax.experimental import pallas as pl
from jax.experimental.pallas import tpu as pltpu

F32 = jnp.float32
BF16 = jnp.bfloat16
HIGHEST = lax.Precision.HIGHEST

D_MODEL = 4096
DEPTH = 2
N_META = 16
SSD_HEADDIM = 64
SSD_INNER = D_MODEL
SSD_HEADS = SSD_INNER // SSD_HEADDIM
SSD_GROUPS = 8
SSD_HPG = SSD_HEADS // SSD_GROUPS
SSD_STATE = 128
SSD_CONV = 4
SSD_GROUP_W = SSD_HPG * SSD_HEADDIM
SSD_CONV_CH = SSD_INNER + 2 * SSD_GROUPS * SSD_STATE
FOX_HEADDIM = 128
FOX_WIDTH = D_MODEL // 2
FOX_HEADS = FOX_WIDTH // FOX_HEADDIM
NEG_INF = -1e30
GLA_HEADS = D_MODEL // 512
GLA_DK = 256
GLA_DV = 512
GLA_KWIDTH = GLA_HEADS * GLA_DK
GLA_VWIDTH = GLA_HEADS * GLA_DV
GLA_RANK = 16
GLA_TAU = 16.0
GLA_CHUNK = 64
CONV_CH = D_MODEL // 2
CONV_WIDTH = 31
D_FF = -(-8 * D_MODEL // (3 * 256)) * 256
LN_EPS = 1e-5

LANES = 128
TBLK = 128
CONV_HALO = 32
MIB = 1024 * 1024


def _cparams(semantics, vmem_mib):
    return pltpu.CompilerParams(dimension_semantics=semantics, vmem_limit_bytes=vmem_mib * MIB)


def _tile(dim, target):
    return dim if dim < target else target


def _sigmoid(x):
    return 1.0 / (1.0 + jnp.exp(-x))


def _silu(x):
    return x * _sigmoid(x)


def _softplus(x):
    return jnp.maximum(x, 0.0) + jnp.log1p(jnp.exp(-jnp.abs(x)))


def _log_sigmoid(x):
    return jnp.minimum(x, 0.0) - jnp.log1p(jnp.exp(-jnp.abs(x)))


def _split_dot(v, onehot, parts):
    out = None
    rem = v
    for _ in range(parts):
        piece = rem.astype(BF16)
        term = jnp.dot(piece, onehot, preferred_element_type=F32)
        out = term if out is None else out + term
        rem = rem - piece.astype(F32)
    return out


def _cumsum_rows(x):
    n = x.shape[0]
    row = lax.broadcasted_iota(jnp.int32, (n, 1), 0)
    shift = 1
    while shift < n:
        x = x + jnp.where(row >= shift, pltpu.roll(x, shift, axis=0), 0.0)
        shift *= 2
    return x


def _tril(n):
    r = lax.broadcasted_iota(jnp.int32, (n, n), 0)
    c = lax.broadcasted_iota(jnp.int32, (n, n), 1)
    return c <= r


def _mm_kernel(x_ref, w_ref, o_ref):
    o_ref[...] = jnp.dot(x_ref[...], w_ref[...].astype(BF16), preferred_element_type=F32).astype(o_ref.dtype)


def _mm_nt_kernel(x_ref, wt_ref, o_ref):
    o_ref[...] = lax.dot_general(x_ref[...], wt_ref[...].astype(BF16), (((1,), (1,)), ((), ())),
                                 preferred_element_type=F32).astype(o_ref.dtype)


def _residual(y_ref, st_ref, g_ref, b_ref):
    st = st_ref[...]
    return (y_ref[...] - st[:, 0:1]) * st[:, 1:2] * g_ref[...] + b_ref[...]


def _residual_specs(bm, bn):
    return [pl.BlockSpec((bm, bn), lambda i, j: (i, j)), pl.BlockSpec((bm, LANES), lambda i, j: (i, 0)),
            pl.BlockSpec((1, bn), lambda i, j: (0, j)), pl.BlockSpec((1, bn), lambda i, j: (0, j))]


def _mm_res_kernel(x_ref, w_ref, y_ref, st_ref, g_ref, b_ref, o_ref, *, alpha):
    acc = jnp.dot(x_ref[...], w_ref[...].astype(BF16), preferred_element_type=F32)
    o_ref[...] = alpha * _residual(y_ref, st_ref, g_ref, b_ref) + acc


def _matmul(x, w, layer, *, n_out, bm, bn, out_dtype, vmem_mib, name, res=None, alpha=None, w_is_nk=False):
    m, k = x.shape
    bm = _tile(m, bm)
    tile = pl.BlockSpec((bm, bn), lambda i, j: (i, j))
    if w_is_nk:
        body, wspec = _mm_nt_kernel, pl.BlockSpec((None, bn, k), lambda i, j: (layer, j, 0))
    else:
        body, wspec = _mm_kernel, pl.BlockSpec((None, k, bn), lambda i, j: (layer, 0, j))
    in_specs = [pl.BlockSpec((bm, k), lambda i, j: (i, 0)), wspec]
    args = (x, w)
    if res is not None:
        assert not w_is_nk
        body, args = functools.partial(_mm_res_kernel, alpha=alpha), (x, w, *res)
        in_specs += _residual_specs(bm, bn)
    return pl.pallas_call(
        body,
        grid=(pl.cdiv(m, bm), n_out // bn),
        in_specs=in_specs,
        out_specs=tile,
        out_shape=jax.ShapeDtypeStruct((m, n_out), out_dtype),
        compiler_params=_cparams(("parallel", "arbitrary"), vmem_mib),
        name=name,
    )(*args)


def _mm2_kernel(a_ref, c_ref, w1_ref, w2_ref, y_ref, st_ref, g_ref, b_ref, o_ref, *, alpha):
    acc = jnp.dot(a_ref[...], w1_ref[...].astype(BF16), preferred_element_type=F32)
    acc = acc + jnp.dot(c_ref[...], w2_ref[...].astype(BF16), preferred_element_type=F32)
    o_ref[...] = alpha * _residual(y_ref, st_ref, g_ref, b_ref) + acc


def _matmul2(a, c, w, layer, res, *, alpha, bm, bn, vmem_mib, name):
    m, k1 = a.shape
    k2 = c.shape[1]
    n = w.shape[2]
    bm = _tile(m, bm)
    return pl.pallas_call(
        functools.partial(_mm2_kernel, alpha=alpha),
        grid=(pl.cdiv(m, bm), n // bn),
        in_specs=[pl.BlockSpec((bm, k1), lambda i, j: (i, 0)),
                  pl.BlockSpec((bm, k2), lambda i, j: (i, 0)),
                  pl.BlockSpec((None, k1, bn), lambda i, j: (layer, 0, j)),
                  pl.BlockSpec((None, k2, bn), lambda i, j: (layer, k1 // k2, j))] + _residual_specs(bm, bn),
        out_specs=pl.BlockSpec((bm, bn), lambda i, j: (i, j)),
        out_shape=jax.ShapeDtypeStruct((m, n), F32),
        compiler_params=_cparams(("parallel", "arbitrary"), vmem_mib),
        name=name,
    )(a, c, w, w, *res)


def _swiglu_up_kernel(x_ref, wg_ref, wu_ref, o_ref):
    x = x_ref[...]
    g = jnp.dot(x, wg_ref[...].astype(BF16), preferred_element_type=F32)
    u = jnp.dot(x, wu_ref[...].astype(BF16), preferred_element_type=F32)
    o_ref[...] = (_silu(g) * u).astype(o_ref.dtype)


def _swiglu_up(x, wg, wu, layer, *, bm, bn, vmem_mib, name):
    m, k = x.shape
    n = wg.shape[2]
    bm = _tile(m, bm)
    wspec = pl.BlockSpec((None, k, bn), lambda i, j: (layer, 0, j))
    return pl.pallas_call(
        _swiglu_up_kernel,
        grid=(pl.cdiv(m, bm), n // bn),
        in_specs=[pl.BlockSpec((bm, k), lambda i, j: (i, 0)), wspec, wspec],
        out_specs=pl.BlockSpec((bm, bn), lambda i, j: (i, j)),
        out_shape=jax.ShapeDtypeStruct((m, n), BF16),
        compiler_params=_cparams(("parallel", "arbitrary"), vmem_mib),
        name=name,
    )(x, wg, wu)


LN_ROWS = 256


def _row_stats(y):
    mean = jnp.mean(y, axis=-1, keepdims=True)
    yc = y - mean
    var = jnp.mean(yc * yc, axis=-1, keepdims=True)
    return mean, lax.rsqrt(var + LN_EPS)


def _ln_kernel(y_ref, g_ref, b_ref, ob_ref, st_ref):
    y = y_ref[...]
    mean, rstd = _row_stats(y)
    ob_ref[...] = ((y - mean) * rstd * g_ref[...] + b_ref[...]).astype(BF16)
    lane = lax.broadcasted_iota(jnp.int32, st_ref.shape, 1)
    st_ref[...] = jnp.where(lane == 0, mean, jnp.where(lane == 1, rstd, 0.0))


def _ln(y, g, b, *, name):
    m, d = y.shape
    bm = _tile(m, LN_ROWS)
    row = pl.BlockSpec((bm, d), lambda i: (i, 0))
    vec = pl.BlockSpec((1, d), lambda i: (0, 0))
    return pl.pallas_call(
        _ln_kernel,
        grid=(pl.cdiv(m, bm),),
        in_specs=[row, vec, vec],
        out_specs=[row, pl.BlockSpec((bm, LANES), lambda i: (i, 0))],
        out_shape=[jax.ShapeDtypeStruct((m, d), BF16), jax.ShapeDtypeStruct((m, LANES), F32)],
        compiler_params=_cparams(("parallel",), 32),
        name=name,
    )(y, g.reshape(1, d), b.reshape(1, d))


def _ln_drop_kernel(ya_ref, yb_ref, g_ref, b_ref, o_ref, *, drop):
    y = jnp.concatenate([ya_ref[0, drop:, :], yb_ref[0]], axis=0)
    mean, rstd = _row_stats(y)
    o_ref[0] = (y - mean) * rstd * g_ref[...] + b_ref[...]


def _ln_drop(y, g, b, *, drop, name):
    bsz, seq_len, d = y.shape
    nt = (seq_len - drop) // TBLK
    assert (seq_len - drop) % TBLK == 0 and TBLK % drop == 0 and drop % 8 == 0
    vec = pl.BlockSpec((1, d), lambda bi, t: (0, 0))
    return pl.pallas_call(
        functools.partial(_ln_drop_kernel, drop=drop),
        grid=(bsz, nt),
        in_specs=[pl.BlockSpec((1, TBLK, d), lambda bi, t: (bi, t, 0)),
                  pl.BlockSpec((1, drop, d), lambda bi, t: (bi, (t + 1) * (TBLK // drop), 0)),
                  vec, vec],
        out_specs=pl.BlockSpec((1, TBLK, d), lambda bi, t: (bi, t, 0)),
        out_shape=jax.ShapeDtypeStruct((bsz, seq_len - drop, d), F32),
        compiler_params=_cparams(("parallel", "parallel"), 32),
        name=name,
    )(y, y, g.reshape(1, d), b.reshape(1, d))


def _ssd_kernel(xs_ref, bc_ref, z_ref, dt_ref, e_ref, cwx_ref, cwb_ref, cbx_ref, cbb_ref,
                dtb_ref, alog_ref, dexp_ref, nw_ref, y_ref,
                s_ref, extx_ref, extb_ref, actx_ref, actb_ref, *, seq_len):
    c = pl.program_id(1)

    @pl.when(c == 0)
    def _():
        s_ref[...] = jnp.zeros_like(s_ref)
        extx_ref[0:8, :] = jnp.zeros((8, SSD_INNER), F32)
        extb_ref[0:8, :] = jnp.zeros((8, SSD_CONV_CH - SSD_INNER), F32)

    valid = seq_len - c * TBLK
    rmask = lax.broadcasted_iota(jnp.int32, (TBLK, 1), 0) < valid

    def conv_silu(raw_ref, ext_ref, w_ref, bias_ref, act_ref):
        raw = jnp.where(rmask, raw_ref[0], 0.0)
        ext_ref[8:8 + TBLK, :] = raw
        acc = bias_ref[...] + w_ref[SSD_CONV - 1:SSD_CONV, :] * raw
        for k in range(SSD_CONV - 1):
            lo = 8 - (SSD_CONV - 1) + k
            acc = acc + w_ref[k:k + 1, :] * ext_ref[lo:lo + TBLK, :]
        ext_ref[0:8, :] = raw[TBLK - 8:TBLK, :]
        act_ref[...] = _silu(acc)

    conv_silu(xs_ref, extx_ref, cwx_ref, cbx_ref, actx_ref)
    conv_silu(bc_ref, extb_ref, cwb_ref, cbb_ref, actb_ref)

    dt_all = jnp.where(rmask, _softplus(dt_ref[0] + dtb_ref[...]), 0.0)
    la_all = dt_all * (-jnp.exp(alog_ref[...]))
    tri = _tril(TBLK)
    a_cs = jnp.dot(tri.astype(F32), la_all, precision=HIGHEST, preferred_element_type=F32)
    a_cs_t = a_cs.T
    e_in = jnp.exp(a_cs)
    e_out = jnp.exp(a_cs[TBLK - 1:TBLK, :] - a_cs)
    stack = jnp.concatenate([dt_all, e_in, e_out], axis=0)
    stack_hi = stack.astype(BF16)
    stack_lo = (stack - stack_hi.astype(F32)).astype(BF16)
    lane = lax.broadcasted_iota(jnp.int32, (TBLK, LANES), 1)
    gw, st = SSD_GROUP_W, SSD_STATE
    for g in range(SSD_GROUPS):
        eg = e_ref[:, g * gw:(g + 1) * gw]
        expanded = (jnp.dot(stack_hi, eg, preferred_element_type=F32)
                    + jnp.dot(stack_lo, eg, preferred_element_type=F32))
        dt_e = expanded[0:TBLK]
        ein_e = expanded[TBLK:2 * TBLK]
        eout_e = expanded[2 * TBLK:3 * TBLK]
        xs = actx_ref[:, g * gw:(g + 1) * gw]
        bb = actb_ref[:, g * st:(g + 1) * st].astype(BF16)
        cc = actb_ref[:, (SSD_GROUPS + g) * st:(SSD_GROUPS + g + 1) * st].astype(BF16)
        xdt = xs * dt_e
        state = s_ref[g]
        y = jnp.dot(cc, state.astype(BF16), preferred_element_type=F32) * ein_e
        cb = lax.dot_general(cc, bb, (((1,), (1,)), ((), ())), preferred_element_type=F32)
        cols = []
        for m in range(gw // LANES):
            xcol = xdt[:, m * LANES:(m + 1) * LANES]
            ycol = None
            for half in range(LANES // SSD_HEADDIM):
                hh = g * SSD_HPG + m * (LANES // SSD_HEADDIM) + half
                colb = jnp.broadcast_to(a_cs[:, hh:hh + 1], (TBLK, TBLK))
                rowb = jnp.broadcast_to(a_cs_t[hh:hh + 1, :], (TBLK, TBLK))
                lmat = jnp.where(tri, jnp.exp(colb - rowb), 0.0) * cb
                in_half = (lane >= half * SSD_HEADDIM) & (lane < (half + 1) * SSD_HEADDIM)
                xh = jnp.where(in_half, xcol, 0.0).astype(BF16)
                term = jnp.dot(lmat.astype(BF16), xh, preferred_element_type=F32)
                ycol = term if ycol is None else ycol + term
            cols.append(ycol)
        y = y + jnp.concatenate(cols, axis=1) + xs * dexp_ref[:, g * gw:(g + 1) * gw]

        xd = (xdt * eout_e).astype(BF16)
        s_ref[g] = state * ein_e[TBLK - 1:TBLK, :] + lax.dot_general(
            bb, xd, (((0,), (0,)), ((), ())), preferred_element_type=F32)

        yz = y * _silu(z_ref[0, :, g * gw:(g + 1) * gw])
        ms = jnp.mean(yz * yz, axis=-1, keepdims=True)
        y_ref[0, :, g * gw:(g + 1) * gw] = (
            yz * lax.rsqrt(ms + LN_EPS) * nw_ref[:, g * gw:(g + 1) * gw]).astype(y_ref.dtype)


def _ssd(p, conv_w, conv_b, dt_bias, a_log, d_skip, norm_w, *, xbc_col, z_col, dt_col):
    bsz, seq_len, _ = p.shape
    nc = pl.cdiv(seq_len, TBLK)
    bcw = SSD_CONV_CH - SSD_INNER
    xoff, boff, zoff, doff = xbc_col // SSD_INNER, (xbc_col + SSD_INNER) // bcw, z_col // SSD_INNER, dt_col // LANES
    expand = ((jnp.arange(SSD_INNER)[None, :] // SSD_HEADDIM) == jnp.arange(LANES)[:, None]).astype(BF16)
    pad = LANES - SSD_HEADS
    dtb = jnp.pad(dt_bias, (0, pad)).reshape(1, LANES)
    alog = jnp.pad(a_log, (0, pad)).reshape(1, LANES)
    dexp = jnp.repeat(d_skip, SSD_HEADDIM).reshape(1, SSD_INNER)
    cb2 = conv_b.reshape(1, SSD_CONV_CH)
    const = lambda shape, col=0: pl.BlockSpec(shape, lambda b, c: (0, col))
    return pl.pallas_call(
        functools.partial(_ssd_kernel, seq_len=seq_len),
        grid=(bsz, nc),
        in_specs=[pl.BlockSpec((1, TBLK, SSD_INNER), lambda b, c: (b, c, xoff)),
                  pl.BlockSpec((1, TBLK, bcw), lambda b, c: (b, c, boff)),
                  pl.BlockSpec((1, TBLK, SSD_INNER), lambda b, c: (b, c, zoff)),
                  pl.BlockSpec((1, TBLK, LANES), lambda b, c: (b, c, doff)),
                  const((LANES, SSD_INNER)),
                  const((SSD_CONV, SSD_INNER)), const((SSD_CONV, bcw), SSD_INNER // bcw),
                  const((1, SSD_INNER)), const((1, bcw), SSD_INNER // bcw),
                  const((1, LANES)), const((1, LANES)),
                  const((1, SSD_INNER)), const((1, SSD_INNER))],
        out_specs=pl.BlockSpec((1, TBLK, SSD_INNER), lambda b, c: (b, c, 0)),
        out_shape=jax.ShapeDtypeStruct((bsz, seq_len, SSD_INNER), BF16),
        scratch_shapes=[pltpu.VMEM((SSD_GROUPS, SSD_STATE, SSD_GROUP_W), F32),
                        pltpu.VMEM((8 + TBLK, SSD_INNER), F32),
                        pltpu.VMEM((8 + TBLK, bcw), F32),
                        pltpu.VMEM((TBLK, SSD_INNER), F32),
                        pltpu.VMEM((TBLK, bcw), F32)],
        compiler_params=_cparams(("parallel", "arbitrary"), 48),
        name="ssd_scan",
    )(p, p, p, p, expand, conv_w, conv_w, cb2, cb2, dtb, alog, dexp, norm_w.reshape(1, SSD_INNER))


FOX_QTILE = 512


def _fox_cum_kernel(p_ref, b_ref, o_ref, *, seq_len):
    nfull, tail = seq_len // TBLK, seq_len % TBLK
    r = lax.broadcasted_iota(jnp.int32, (TBLK, TBLK), 0)
    c = lax.broadcasted_iota(jnp.int32, (TBLK, TBLK), 1)
    upper = (r <= c).astype(F32)
    carry = jnp.zeros((LANES, 1), F32)
    for blk in range(nfull + (1 if tail else 0)):
        rows = TBLK if blk < nfull else tail
        lf = _log_sigmoid(p_ref[0, blk * TBLK:blk * TBLK + rows, :] + b_ref[...])
        if rows < TBLK:
            lf = jnp.concatenate([lf, jnp.zeros((TBLK - rows, LANES), F32)], axis=0)
        cs = jnp.dot(lf.T, upper, precision=HIGHEST, preferred_element_type=F32) + carry
        o_ref[0, :, blk * TBLK:(blk + 1) * TBLK] = cs
        carry = cs[:, TBLK - 1:TBLK]


def _fox_cum(p, fgate_b, *, f_col):
    bsz, seq_len, _ = p.shape
    lp = pl.cdiv(seq_len, TBLK) * TBLK
    foff = f_col // LANES
    bias = jnp.pad(fgate_b, (0, LANES - FOX_HEADS)).reshape(1, LANES)
    return pl.pallas_call(
        functools.partial(_fox_cum_kernel, seq_len=seq_len),
        grid=(bsz,),
        in_specs=[pl.BlockSpec((1, seq_len, LANES), lambda b: (b, 0, foff)),
                  pl.BlockSpec((1, LANES), lambda b: (0, 0))],
        out_specs=pl.BlockSpec((1, LANES, lp), lambda b: (b, 0, 0)),
        out_shape=jax.ShapeDtypeStruct((bsz, LANES, lp), F32),
        compiler_params=_cparams(("parallel",), 32),
        name="fox_cum",
    )(p, bias)


def _fox_tiles(seq_len):
    aligned = (seq_len // TBLK) * TBLK
    tiles, start = [], 0
    while start < aligned:
        rows = min(FOX_QTILE, aligned - start)
        tiles.append((start, rows, start + rows))
        start += rows
    if seq_len > aligned:
        tiles.append((aligned, seq_len - aligned, aligned + TBLK))
    return tiles


def _fox_kernel(q_ref, k_ref, v_ref, ck_ref, o_ref, kt_ref, vb_ref, *, seq_len):
    for lo in range(0, seq_len, TBLK):
        rows = min(TBLK, seq_len - lo)
        kb = k_ref[0, lo:lo + rows, :]
        vb = v_ref[0, lo:lo + rows, :]
        if rows < TBLK:
            zpad = jnp.zeros((TBLK - rows, FOX_HEADDIM), F32)
            kb = jnp.concatenate([kb, zpad], axis=0)
            vb = jnp.concatenate([vb, zpad], axis=0)
        kt_ref[:, lo:lo + TBLK] = kb.T.astype(BF16)
        vb_ref[lo:lo + TBLK, :] = vb.astype(BF16)

    for start, rows, lk in _fox_tiles(seq_len):
        q = (q_ref[0, start:start + rows, :] * (FOX_HEADDIM ** -0.5)).astype(BF16)
        dw = lk - start
        r = lax.broadcasted_iota(jnp.int32, (rows, dw), 0)
        c = lax.broadcasted_iota(jnp.int32, (rows, dw), 1)
        sd = jnp.dot(q, kt_ref[:, start:lk], preferred_element_type=F32) - ck_ref[0, 0, :, start:lk]
        sd = jnp.where(c <= r, sd, NEG_INF)
        m = jnp.max(sd, axis=-1, keepdims=True)
        if start:
            so = jnp.dot(q, kt_ref[:, 0:start], preferred_element_type=F32) - ck_ref[0, 0, :, 0:start]
            m = jnp.maximum(m, jnp.max(so, axis=-1, keepdims=True))
        pd = jnp.exp(sd - m)
        l = jnp.sum(pd, axis=-1, keepdims=True)
        o = jnp.dot(pd.astype(BF16), vb_ref[start:lk, :], preferred_element_type=F32)
        if start:
            po = jnp.exp(so - m)
            l = l + jnp.sum(po, axis=-1, keepdims=True)
            o = o + jnp.dot(po.astype(BF16), vb_ref[0:start, :], preferred_element_type=F32)
        o_ref[0, start:start + rows, :] = (o / l).astype(o_ref.dtype)


def _fox(p, cum_t, *, q_col):
    bsz, seq_len, _ = p.shape
    lp = cum_t.shape[2]
    qo = q_col // FOX_HEADDIM
    ko = qo + FOX_HEADS
    vo = ko + FOX_HEADS
    ck = cum_t.reshape(bsz, LANES, 1, lp)
    return pl.pallas_call(
        functools.partial(_fox_kernel, seq_len=seq_len),
        grid=(bsz, FOX_HEADS),
        in_specs=[pl.BlockSpec((1, seq_len, FOX_HEADDIM), lambda b, h: (b, 0, qo + h)),
                  pl.BlockSpec((1, seq_len, FOX_HEADDIM), lambda b, h: (b, 0, ko + h)),
                  pl.BlockSpec((1, seq_len, FOX_HEADDIM), lambda b, h: (b, 0, vo + h)),
                  pl.BlockSpec((1, 1, 1, lp), lambda b, h: (b, h, 0, 0))],
        out_specs=pl.BlockSpec((1, seq_len, FOX_HEADDIM), lambda b, h: (b, 0, h)),
        out_shape=jax.ShapeDtypeStruct((bsz, seq_len, FOX_WIDTH), BF16),
        scratch_shapes=[pltpu.VMEM((FOX_HEADDIM, lp), BF16),
                        pltpu.VMEM((lp, FOX_HEADDIM), BF16)],
        compiler_params=_cparams(("parallel", "parallel"), 48),
        name="fox_attn",
    )(p, p, p, ck)


GLA_TBLK = 256
GLA_HPS = 4


def _gla_kernel(q_ref, k_ref, v_ref, r_ref, glr_ref, wgu_ref, gb_ref, nw_ref, o_ref, st_ref, *, seq_len):
    t = pl.program_id(2)

    @pl.when(t == 0)
    def _():
        st_ref[...] = jnp.zeros_like(st_ref)

    valid = seq_len - t * GLA_TBLK
    rmask = lax.broadcasted_iota(jnp.int32, (GLA_TBLK, 1), 0) < valid
    glr = glr_ref[0].astype(BF16)
    tri = _tril(GLA_CHUNK)
    mid = GLA_CHUNK // 2
    for hd in range(GLA_HPS):
        ks = slice(hd * GLA_DK, (hd + 1) * GLA_DK)
        vs = slice(hd * GLA_DV, (hd + 1) * GLA_DV)
        gpre = jnp.dot(glr, wgu_ref[:, ks], preferred_element_type=F32) + gb_ref[:, ks]
        g_all = jnp.where(rmask, _log_sigmoid(gpre) * (1.0 / GLA_TAU), 0.0)
        q_all = jnp.where(rmask, q_ref[0, :, ks], 0.0) * (GLA_DK ** -0.5)
        k_all = jnp.where(rmask, k_ref[0, :, ks], 0.0)
        v_all = jnp.where(rmask, v_ref[0, :, vs], 0.0)
        outs = []
        for ci in range(GLA_TBLK // GLA_CHUNK):
            sl = slice(ci * GLA_CHUNK, (ci + 1) * GLA_CHUNK)
            qc, kc, vc, gc = q_all[sl], k_all[sl], v_all[sl], g_all[sl]
            vcb = vc.astype(BF16)
            bcum = _cumsum_rows(gc)
            bref = bcum[mid:mid + 1, :]
            btot = bcum[GLA_CHUNK - 1:GLA_CHUNK, :]
            qe = (qc * jnp.exp(bcum - bref)).astype(BF16)
            ke = (kc * jnp.exp(bref - bcum)).astype(BF16)
            scores = lax.dot_general(qe, ke, (((1,), (1,)), ((), ())), preferred_element_type=F32)
            scores = jnp.where(tri, scores, 0.0).astype(BF16)
            state_t = st_ref[hd]
            qd = (qc * jnp.exp(bcum)).astype(BF16)
            o = jnp.dot(scores, vcb, preferred_element_type=F32) + lax.dot_general(
                qd, state_t.astype(BF16), (((1,), (1,)), ((), ())), preferred_element_type=F32)
            kd = (kc * jnp.exp(btot - bcum)).astype(BF16)
            st_ref[hd] = state_t * jnp.exp(btot) + lax.dot_general(
                vcb, kd, (((0,), (0,)), ((), ())), preferred_element_type=F32)
            outs.append(o)
        o = jnp.concatenate(outs, axis=0)
        ms = jnp.mean(o * o, axis=-1, keepdims=True)
        o = o * lax.rsqrt(ms + LN_EPS) * nw_ref[...]
        o_ref[0, :, vs] = (o * _silu(r_ref[0, :, vs])).astype(o_ref.dtype)


def _gla(p, pr, w_gate_up, gate_b, norm_w, *, q_col, k_col, v_col, glr_col, r_col):
    bsz, seq_len, _ = p.shape
    nt = pl.cdiv(seq_len, GLA_TBLK)
    kw, vw = GLA_HPS * GLA_DK, GLA_HPS * GLA_DV
    qo, ko, vo, ro, go = q_col // kw, k_col // kw, v_col // vw, r_col // vw, glr_col // LANES
    wgu = jnp.pad(w_gate_up, ((0, LANES - GLA_RANK), (0, 0))).astype(BF16)
    return pl.pallas_call(
        functools.partial(_gla_kernel, seq_len=seq_len),
        grid=(bsz, GLA_HEADS // GLA_HPS, nt),
        in_specs=[pl.BlockSpec((1, GLA_TBLK, kw), lambda b, h, t: (b, t, qo + h)),
                  pl.BlockSpec((1, GLA_TBLK, kw), lambda b, h, t: (b, t, ko + h)),
                  pl.BlockSpec((1, GLA_TBLK, vw), lambda b, h, t: (b, t, vo + h)),
                  pl.BlockSpec((1, GLA_TBLK, vw), lambda b, h, t: (b, t, ro + h)),
                  pl.BlockSpec((1, GLA_TBLK, LANES), lambda b, h, t: (b, t, go)),
                  pl.BlockSpec((LANES, kw), lambda b, h, t: (0, h)),
                  pl.BlockSpec((1, kw), lambda b, h, t: (0, h)),
                  pl.BlockSpec((1, GLA_DV), lambda b, h, t: (0, 0))],
        out_specs=pl.BlockSpec((1, GLA_TBLK, vw), lambda b, h, t: (b, t, h)),
        out_shape=jax.ShapeDtypeStruct((bsz, seq_len, GLA_VWIDTH), BF16),
        scratch_shapes=[pltpu.VMEM((GLA_HPS, GLA_DV, GLA_DK), F32)],
        compiler_params=_cparams(("parallel", "parallel", "arbitrary"), 40),
        name="gla_scan",
    )(p, p, p, pr, p, wgu, gate_b.reshape(1, GLA_KWIDTH), norm_w.reshape(1, GLA_DV))


CONV_CBLK = 256


def _conf_kernel(ua_ref, ug_ref, w_ref, b_ref, g_ref, be_ref, o_ref, ext_ref, acc_ref, sh_ref, *, seq_len):
    t = pl.program_id(1)

    @pl.when(t == 0)
    def _():
        ext_ref[0:CONV_HALO, :] = jnp.zeros((CONV_HALO, CONV_CH), F32)

    valid = seq_len - t * TBLK
    rmask = lax.broadcasted_iota(jnp.int32, (TBLK, 1), 0) < valid
    glu = jnp.where(rmask, ua_ref[0] * _sigmoid(ug_ref[0]), 0.0)
    ext_ref[CONV_HALO:CONV_HALO + TBLK, :] = glu
    base = CONV_HALO - (CONV_WIDTH - 1)
    for cb in range(CONV_CH // CONV_CBLK):
        cs = slice(cb * CONV_CBLK, (cb + 1) * CONV_CBLK)
        acc = jnp.broadcast_to(b_ref[:, cs], (TBLK, CONV_CBLK))
        for r in range(8):
            taps = [k for k in range(CONV_WIDTH) if (base + k) % 8 == r]
            span = 8 * max((base + k) // 8 for k in taps) + TBLK
            sh_ref[0:span, :] = ext_ref[r:r + span, cs]
            for k in taps:
                lo = 8 * ((base + k) // 8)
                acc = acc + w_ref[k:k + 1, cs] * sh_ref[lo:lo + TBLK, :]
        acc_ref[:, cs] = acc
    ext_ref[0:CONV_HALO, :] = ext_ref[TBLK:TBLK + CONV_HALO, :]
    u = acc_ref[...]
    uc = u - jnp.mean(u, axis=-1, keepdims=True)
    var = jnp.mean(uc * uc, axis=-1, keepdims=True)
    o_ref[0] = _silu(uc * lax.rsqrt(var + LN_EPS) * g_ref[...] + be_ref[...]).astype(o_ref.dtype)


def _conformer_conv(p, dw_w, dw_b, ln_g, ln_b, *, glu_col):
    bsz, seq_len, _ = p.shape
    nt = pl.cdiv(seq_len, TBLK)
    ao = glu_col // CONV_CH
    vec = pl.BlockSpec((1, CONV_CH), lambda b, t: (0, 0))
    return pl.pallas_call(
        functools.partial(_conf_kernel, seq_len=seq_len),
        grid=(bsz, nt),
        in_specs=[pl.BlockSpec((1, TBLK, CONV_CH), lambda b, t: (b, t, ao)),
                  pl.BlockSpec((1, TBLK, CONV_CH), lambda b, t: (b, t, ao + 1)),
                  pl.BlockSpec((CONV_WIDTH, CONV_CH), lambda b, t: (0, 0)),
                  vec, vec, vec],
        out_specs=pl.BlockSpec((1, TBLK, CONV_CH), lambda b, t: (b, t, 0)),
        out_shape=jax.ShapeDtypeStruct((bsz, seq_len, CONV_CH), BF16),
        scratch_shapes=[pltpu.VMEM((CONV_HALO + TBLK, CONV_CH), F32),
                        pltpu.VMEM((TBLK, CONV_CH), F32),
                        pltpu.VMEM((CONV_HALO + TBLK, CONV_CBLK), F32)],
        compiler_params=_cparams(("parallel", "arbitrary"), 32),
        name="conformer_conv",
    )(p, p, dw_w, dw_b.reshape(1, CONV_CH), ln_g.reshape(1, CONV_CH), ln_b.reshape(1, CONV_CH))


MM_BM = 1376
MM_BN = 512

EV_Z_COL = 0
EV_XBC_COL = SSD_INNER
EV_DT_COL = SSD_INNER + SSD_CONV_CH
EV_A_COLS = -(-(EV_DT_COL + SSD_HEADS) // MM_BN) * MM_BN
EV_TAIL = EV_DT_COL + SSD_HEADS
EV_B_COLS = 3 * FOX_WIDTH
OD_Q_COL = 0
OD_K_COL = GLA_KWIDTH
OD_V_COL = 2 * GLA_KWIDTH
OD_GLR_COL = 2 * GLA_KWIDTH + GLA_VWIDTH
OD_A_COLS = -(-(OD_GLR_COL + GLA_RANK) // MM_BN) * MM_BN
OD_TAIL = OD_GLR_COL + GLA_RANK
OD_R_COL = 0
OD_GLU_COL = GLA_VWIDTH
OD_B_COLS = GLA_VWIDTH + 2 * CONV_CH


def _normed(y, g, b, name):
    hb, stats = _ln(y, g, b, name=name)
    return hb, (y, stats, g.reshape(1, -1), b.reshape(1, -1))


def _out_proj(a, c, w_out, layer, res, alpha, name):
    m = res[0].shape[0]
    return _matmul2(a.reshape(m, -1), c.reshape(m, -1), w_out, layer, res, alpha=alpha, bm=MM_BM, bn=256,
                    vmem_mib=61, name=name)


def _ffn(res, hb, w_gate, w_up, w_down_b, layer, alpha, name):
    mid = _swiglu_up(hb, w_gate, w_up, layer, bm=MM_BM, bn=256, vmem_mib=48, name=name + "_up")
    return _matmul(mid, w_down_b, layer, n_out=D_MODEL, bm=MM_BM // 2, bn=256, out_dtype=F32, vmem_mib=56,
                   name=name + "_down", res=res, alpha=alpha)


def kernel(x, meta_tokens, ev_w_in, ev_conv_w, ev_conv_b, ev_dt_bias, ev_a_log, ev_d_skip, ev_ssm_norm_w, ev_fgate_b, ev_w_out, od_w_in, od_w_gate_up, od_gate_b, od_gla_norm_w, od_dwconv_w, od_dwconv_b, od_conv_ln_g, od_conv_ln_b, od_w_out, ln_mix_g, ln_mix_b, ffn_w_gate, ffn_w_up, ffn_w_down, ln_ffn_g, ln_ffn_b):
    alpha = (2.0 * DEPTH) ** 0.25
    bsz = x.shape[0]
    meta = jnp.broadcast_to(meta_tokens[None].astype(x.dtype), (bsz, N_META, D_MODEL))
    h3 = jnp.concatenate([meta, x], axis=1)
    seq_len = h3.shape[1]
    h = h3.reshape(bsz * seq_len, D_MODEL)
    hb = h.astype(BF16)
    unit_stats = jnp.zeros((h.shape[0], LANES), F32).at[:, 1].set(1.0)
    res = (h, unit_stats, jnp.ones((1, D_MODEL), F32), jnp.zeros((1, D_MODEL), F32))
    w_down_b = ffn_w_down.astype(BF16)
    for i in range(DEPTH):
        j = i // 2
        if i % 2 == 0:
            w_nk = jnp.swapaxes(ev_w_in, 1, 2)
            w_qkv = w_nk[j, EV_TAIL:EV_TAIL + EV_B_COLS].astype(BF16)[None]
            w_f = jnp.pad(w_nk[j, EV_TAIL + EV_B_COLS:], ((0, LANES - FOX_HEADS), (0, 0))).astype(BF16)[None]
            pa = _matmul(hb, w_nk, j, n_out=EV_A_COLS, bm=MM_BM, bn=MM_BN, out_dtype=F32, vmem_mib=52,
                         name="ev_in_a", w_is_nk=True).reshape(bsz, seq_len, EV_A_COLS)
            pb = _matmul(hb, w_qkv, 0, n_out=EV_B_COLS, bm=MM_BM, bn=MM_BN, out_dtype=F32, vmem_mib=52,
                         name="ev_in_b", w_is_nk=True).reshape(bsz, seq_len, EV_B_COLS)
            pf = _matmul(hb, w_f, 0, n_out=LANES, bm=MM_BM, bn=LANES, out_dtype=F32, vmem_mib=52,
                         name="ev_in_f", w_is_nk=True).reshape(bsz, seq_len, LANES)
            y = _ssd(pa, ev_conv_w[j], ev_conv_b[j], ev_dt_bias[j], ev_a_log[j], ev_d_skip[j], ev_ssm_norm_w[j],
                     xbc_col=EV_XBC_COL, z_col=EV_Z_COL, dt_col=EV_DT_COL)
            cum_t = _fox_cum(pf, ev_fgate_b[j], f_col=0)
            o = _fox(pb, cum_t, q_col=0)
            y = _out_proj(y, o, ev_w_out, j, res, alpha, "ev_out")
        else:
            w_nk = jnp.swapaxes(od_w_in, 1, 2)
            w_tail = w_nk[j, OD_TAIL:].astype(BF16)[None]
            pa = _matmul(hb, w_nk, j, n_out=OD_A_COLS, bm=MM_BM, bn=MM_BN, out_dtype=F32, vmem_mib=52,
                         name="od_in_a", w_is_nk=True).reshape(bsz, seq_len, OD_A_COLS)
            pb = _matmul(hb, w_tail, 0, n_out=OD_B_COLS, bm=MM_BM, bn=MM_BN, out_dtype=F32, vmem_mib=52,
                         name="od_in_b", w_is_nk=True).reshape(bsz, seq_len, OD_B_COLS)
            o = _gla(pa, pb, od_w_gate_up[j], od_gate_b[j], od_gla_norm_w[j],
                     q_col=OD_Q_COL, k_col=OD_K_COL, v_col=OD_V_COL, glr_col=OD_GLR_COL, r_col=OD_R_COL)
            u = _conformer_conv(pb, od_dwconv_w[j], od_dwconv_b[j], od_conv_ln_g[j], od_conv_ln_b[j],
                                glu_col=OD_GLU_COL)
            y = _out_proj(o, u, od_w_out, j, res, alpha, "od_out")
        hb, res = _normed(y, ln_mix_g[i], ln_mix_b[i], "mix_ln")
        y = _ffn(res, hb, ffn_w_gate, ffn_w_up, w_down_b, i, alpha, "ffn")
        if i + 1 < DEPTH:
            hb, res = _normed(y, ln_ffn_g[i], ln_ffn_b[i], "ffn_ln")
    return _ln_drop(y.reshape(bsz, seq_len, D_MODEL), ln_ffn_g[DEPTH - 1], ln_ffn_b[DEPTH - 1], drop=N_META,
                    name="final_ln")
```

```python
import functools

import jax
import jax.numpy as jnp
from jax import lax
from jax.experimental import pallas as pl
from jax.experimental.pallas import tpu as pltpu

F32 = jnp.float32
BF16 = jnp.bfloat16
HIGHEST = lax.Precision.HIGHEST

D_MODEL = 4096
DEPTH = 2
N_META = 16
SSD_HEADDIM = 64
SSD_INNER = D_MODEL
SSD_HEADS = SSD_INNER // SSD_HEADDIM
SSD_GROUPS = 8
SSD_HPG = SSD_HEADS // SSD_GROUPS
SSD_STATE = 128
SSD_CONV = 4
SSD_GROUP_W = SSD_HPG * SSD_HEADDIM
SSD_CONV_CH = SSD_INNER + 2 * SSD_GROUPS * SSD_STATE
FOX_HEADDIM = 128
FOX_WIDTH = D_MODEL // 2
FOX_HEADS = FOX_WIDTH // FOX_HEADDIM
NEG_INF = -1e30
GLA_HEADS = D_MODEL // 512
GLA_DK = 256
GLA_DV = 512
GLA_KWIDTH = GLA_HEADS * GLA_DK
GLA_VWIDTH = GLA_HEADS * GLA_DV
GLA_RANK = 16
GLA_TAU = 16.0
GLA_CHUNK = 64
CONV_CH = D_MODEL // 2
CONV_WIDTH = 31
D_FF = -(-8 * D_MODEL // (3 * 256)) * 256
LN_EPS = 1e-5

LANES = 128
TBLK = 128
CONV_HALO = 32
MIB = 1024 * 1024


def _cparams(semantics, vmem_mib):
    return pltpu.CompilerParams(dimension_semantics=semantics, vmem_limit_bytes=vmem_mib * MIB)


def _tile(dim, target):
    return dim if dim < target else target


def _sigmoid(x):
    return 1.0 / (1.0 + jnp.exp(-x))


def _silu(x):
    return x * _sigmoid(x)


def _softplus(x):
    return jnp.maximum(x, 0.0) + jnp.log1p(jnp.exp(-jnp.abs(x)))


def _log_sigmoid(x):
    return jnp.minimum(x, 0.0) - jnp.log1p(jnp.exp(-jnp.abs(x)))


def _split_dot(v, onehot, parts):
    out = None
    rem = v
    for _ in range(parts):
        piece = rem.astype(BF16)
        term = jnp.dot(piece, onehot, preferred_element_type=F32)
        out = term if out is None else out + term
        rem = rem - piece.astype(F32)
    return out


def _cumsum_rows(x):
    n = x.shape[0]
    row = lax.broadcasted_iota(jnp.int32, (n, 1), 0)
    shift = 1
    while shift < n:
        x = x + jnp.where(row >= shift, pltpu.roll(x, shift, axis=0), 0.0)
        shift *= 2
    return x


def _tril(n):
    r = lax.broadcasted_iota(jnp.int32, (n, n), 0)
    c = lax.broadcasted_iota(jnp.int32, (n, n), 1)
    return c <= r


def _mm_kernel(x_ref, w_ref, o_ref):
    o_ref[...] = jnp.dot(x_ref[...], w_ref[...].astype(BF16), preferred_element_type=F32).astype(o_ref.dtype)


def _mm_nt_kernel(x_ref, wt_ref, o_ref):
    o_ref[...] = lax.dot_general(x_ref[...], wt_ref[...].astype(BF16), (((1,), (1,)), ((), ())),
                                 preferred_element_type=F32).astype(o_ref.dtype)


def _residual(y_ref, st_ref, g_ref, b_ref):
    st = st_ref[...]
    return (y_ref[...] - st[:, 0:1]) * st[:, 1:2] * g_ref[...] + b_ref[...]


def _residual_specs(bm, bn):
    return [pl.BlockSpec((bm, bn), lambda i, j: (i, j)), pl.BlockSpec((bm, LANES), lambda i, j: (i, 0)),
            pl.BlockSpec((1, bn), lambda i, j: (0, j)), pl.BlockSpec((1, bn), lambda i, j: (0, j))]


def _mm_res_kernel(x_ref, w_ref, y_ref, st_ref, g_ref, b_ref, o_ref, *, alpha):
    acc = jnp.dot(x_ref[...], w_ref[...].astype(BF16), preferred_element_type=F32)
    o_ref[...] = alpha * _residual(y_ref, st_ref, g_ref, b_ref) + acc


def _matmul(x, w, layer, *, n_out, bm, bn, out_dtype, vmem_mib, name, res=None, alpha=None, w_is_nk=False):
    m, k = x.shape
    bm = _tile(m, bm)
    tile = pl.BlockSpec((bm, bn), lambda i, j: (i, j))
    if w_is_nk:
        body, wspec = _mm_nt_kernel, pl.BlockSpec((None, bn, k), lambda i, j: (layer, j, 0))
    else:
        body, wspec = _mm_kernel, pl.BlockSpec((None, k, bn), lambda i, j: (layer, 0, j))
    in_specs = [pl.BlockSpec((bm, k), lambda i, j: (i, 0)), wspec]
    args = (x, w)
    if res is not None:
        assert not w_is_nk
        body, args = functools.partial(_mm_res_kernel, alpha=alpha), (x, w, *res)
        in_specs += _residual_specs(bm, bn)
    return pl.pallas_call(
        body,
        grid=(pl.cdiv(m, bm), n_out // bn),
        in_specs=in_specs,
        out_specs=tile,
        out_shape=jax.ShapeDtypeStruct((m, n_out), out_dtype),
        compiler_params=_cparams(("parallel", "arbitrary"), vmem_mib),
        name=name,
    )(*args)


def _mm2_kernel(a_ref, c_ref, w1_ref, w2_ref, y_ref, st_ref, g_ref, b_ref, o_ref, *, alpha):
    acc = jnp.dot(a_ref[...], w1_ref[...].astype(BF16), preferred_element_type=F32)
    acc = acc + jnp.dot(c_ref[...], w2_ref[...].astype(BF16), preferred_element_type=F32)
    o_ref[...] = alpha * _residual(y_ref, st_ref, g_ref, b_ref) + acc


def _matmul2(a, c, w, layer, res, *, alpha, bm, bn, vmem_mib, name):
    m, k1 = a.shape
    k2 = c.shape[1]
    n = w.shape[2]
    bm = _tile(m, bm)
    return pl.pallas_call(
        functools.partial(_mm2_kernel, alpha=alpha),
        grid=(pl.cdiv(m, bm), n // bn),
        in_specs=[pl.BlockSpec((bm, k1), lambda i, j: (i, 0)),
                  pl.BlockSpec((bm, k2), lambda i, j: (i, 0)),
                  pl.BlockSpec((None, k1, bn), lambda i, j: (layer, 0, j)),
                  pl.BlockSpec((None, k2, bn), lambda i, j: (layer, k1 // k2, j))] + _residual_specs(bm, bn),
        out_specs=pl.BlockSpec((bm, bn), lambda i, j: (i, j)),
        out_shape=jax.ShapeDtypeStruct((m, n), F32),
        compiler_params=_cparams(("parallel", "arbitrary"), vmem_mib),
        name=name,
    )(a, c, w, w, *res)


def _swiglu_up_kernel(x_ref, wg_ref, wu_ref, o_ref):
    x = x_ref[...]
    g = jnp.dot(x, wg_ref[...].astype(BF16), preferred_element_type=F32)
    u = jnp.dot(x, wu_ref[...].astype(BF16), preferred_element_type=F32)
    o_ref[...] = (_silu(g) * u).astype(o_ref.dtype)


def _swiglu_up(x, wg, wu, layer, *, bm, bn, vmem_mib, name):
    m, k = x.shape
    n = wg.shape[2]
    bm = _tile(m, bm)
    wspec = pl.BlockSpec((None, k, bn), lambda i, j: (layer, 0, j))
    return pl.pallas_call(
        _swiglu_up_kernel,
        grid=(pl.cdiv(m, bm), n // bn),
        in_specs=[pl.BlockSpec((bm, k), lambda i, j: (i, 0)), wspec, wspec],
        out_specs=pl.BlockSpec((bm, bn), lambda i, j: (i, j)),
        out_shape=jax.ShapeDtypeStruct((m, n), BF16),
        compiler_params=_cparams(("parallel", "arbitrary"), vmem_mib),
        name=name,
    )(x, wg, wu)


LN_ROWS = 256


def _row_stats(y):
    mean = jnp.mean(y, axis=-1, keepdims=True)
    yc = y - mean
    var = jnp.mean(yc * yc, axis=-1, keepdims=True)
    return mean, lax.rsqrt(var + LN_EPS)


def _ln_kernel(y_ref, g_ref, b_ref, ob_ref, st_ref):
    y = y_ref[...]
    mean, rstd = _row_stats(y)
    ob_ref[...] = ((y - mean) * rstd * g_ref[...] + b_ref[...]).astype(BF16)
    lane = lax.broadcasted_iota(jnp.int32, st_ref.shape, 1)
    st_ref[...] = jnp.where(lane == 0, mean, jnp.where(lane == 1, rstd, 0.0))


def _ln(y, g, b, *, name):
    m, d = y.shape
    bm = _tile(m, LN_ROWS)
    row = pl.BlockSpec((bm, d), lambda i: (i, 0))
    vec = pl.BlockSpec((1, d), lambda i: (0, 0))
    return pl.pallas_call(
        _ln_kernel,
        grid=(pl.cdiv(m, bm),),
        in_specs=[row, vec, vec],
        out_specs=[row, pl.BlockSpec((bm, LANES), lambda i: (i, 0))],
        out_shape=[jax.ShapeDtypeStruct((m, d), BF16), jax.ShapeDtypeStruct((m, LANES), F32)],
        compiler_params=_cparams(("parallel",), 32),
        name=name,
    )(y, g.reshape(1, d), b.reshape(1, d))


def _ln_drop_kernel(ya_ref, yb_ref, g_ref, b_ref, o_ref, *, drop):
    y = jnp.concatenate([ya_ref[0, drop:, :], yb_ref[0]], axis=0)
    mean, rstd = _row_stats(y)
    o_ref[0] = (y - mean) * rstd * g_ref[...] + b_ref[...]


def _ln_drop(y, g, b, *, drop, name):
    bsz, seq_len, d = y.shape
    nt = (seq_len - drop) // TBLK
    assert (seq_len - drop) % TBLK == 0 and TBLK % drop == 0 and drop % 8 == 0
    vec = pl.BlockSpec((1, d), lambda bi, t: (0, 0))
    return pl.pallas_call(
        functools.partial(_ln_drop_kernel, drop=drop),
        grid=(bsz, nt),
        in_specs=[pl.BlockSpec((1, TBLK, d), lambda bi, t: (bi, t, 0)),
                  pl.BlockSpec((1, drop, d), lambda bi, t: (bi, (t + 1) * (TBLK // drop), 0)),
                  vec, vec],
        out_specs=pl.BlockSpec((1, TBLK, d), lambda bi, t: (bi, t, 0)),
        out_shape=jax.ShapeDtypeStruct((bsz, seq_len - drop, d), F32),
        compiler_params=_cparams(("parallel", "parallel"), 32),
        name=name,
    )(y, y, g.reshape(1, d), b.reshape(1, d))


def _ssd_kernel(xs_ref, bc_ref, z_ref, dt_ref, e_ref, cwx_ref, cwb_ref, cbx_ref, cbb_ref,
                dtb_ref, alog_ref, dexp_ref, nw_ref, y_ref,
                s_ref, extx_ref, extb_ref, actx_ref, actb_ref, *, seq_len):
    c = pl.program_id(1)

    @pl.when(c == 0)
    def _():
        s_ref[...] = jnp.zeros_like(s_ref)
        extx_ref[0:8, :] = jnp.zeros((8, SSD_INNER), F32)
        extb_ref[0:8, :] = jnp.zeros((8, SSD_CONV_CH - SSD_INNER), F32)

    valid = seq_len - c * TBLK
    rmask = lax.broadcasted_iota(jnp.int32, (TBLK, 1), 0) < valid

    def conv_silu(raw_ref, ext_ref, w_ref, bias_ref, act_ref):
        raw = jnp.where(rmask, raw_ref[0], 0.0)
        ext_ref[8:8 + TBLK, :] = raw
        acc = bias_ref[...] + w_ref[SSD_CONV - 1:SSD_CONV, :] * raw
        for k in range(SSD_CONV - 1):
            lo = 8 - (SSD_CONV - 1) + k
            acc = acc + w_ref[k:k + 1, :] * ext_ref[lo:lo + TBLK, :]
        ext_ref[0:8, :] = raw[TBLK - 8:TBLK, :]
        act_ref[...] = _silu(acc)

    conv_silu(xs_ref, extx_ref, cwx_ref, cbx_ref, actx_ref)
    conv_silu(bc_ref, extb_ref, cwb_ref, cbb_ref, actb_ref)

    dt_all = jnp.where(rmask, _softplus(dt_ref[0] + dtb_ref[...]), 0.0)
    la_all = dt_all * (-jnp.exp(alog_ref[...]))
    tri = _tril(TBLK)
    a_cs = jnp.dot(tri.astype(F32), la_all, precision=HIGHEST, preferred_element_type=F32)
    a_cs_t = a_cs.T
    e_in = jnp.exp(a_cs)
    e_out = jnp.exp(a_cs[TBLK - 1:TBLK, :] - a_cs)
    stack = jnp.concatenate([dt_all, e_in, e_out], axis=0)
    stack_hi = stack.astype(BF16)
    stack_lo = (stack - stack_hi.astype(F32)).astype(BF16)
    lane = lax.broadcasted_iota(jnp.int32, (TBLK, LANES), 1)
    gw, st = SSD_GROUP_W, SSD_STATE
    for g in range(SSD_GROUPS):
        eg = e_ref[:, g * gw:(g + 1) * gw]
        expanded = (jnp.dot(stack_hi, eg, preferred_element_type=F32)
                    + jnp.dot(stack_lo, eg, preferred_element_type=F32))
        dt_e = expanded[0:TBLK]
        ein_e = expanded[TBLK:2 * TBLK]
        eout_e = expanded[2 * TBLK:3 * TBLK]
        xs = actx_ref[:, g * gw:(g + 1) * gw]
        bb = actb_ref[:, g * st:(g + 1) * st].astype(BF16)
        cc = actb_ref[:, (SSD_GROUPS + g) * st:(SSD_GROUPS + g + 1) * st].astype(BF16)
        xdt = xs * dt_e
        state = s_ref[g]
        y = jnp.dot(cc, state.astype(BF16), preferred_element_type=F32) * ein_e
        cb = lax.dot_general(cc, bb, (((1,), (1,)), ((), ())), preferred_element_type=F32)
        cols = []
        for m in range(gw // LANES):
            xcol = xdt[:, m * LANES:(m + 1) * LANES]
            ycol = None
            for half in range(LANES // SSD_HEADDIM):
                hh = g * SSD_HPG + m * (LANES // SSD_HEADDIM) + half
                colb = jnp.broadcast_to(a_cs[:, hh:hh + 1], (TBLK, TBLK))
                rowb = jnp.broadcast_to(a_cs_t[hh:hh + 1, :], (TBLK, TBLK))
                lmat = jnp.where(tri, jnp.exp(colb - rowb), 0.0) * cb
                in_half = (lane >= half * SSD_HEADDIM) & (lane < (half + 1) * SSD_HEADDIM)
                xh = jnp.where(in_half, xcol, 0.0).astype(BF16)
                term = jnp.dot(lmat.astype(BF16), xh, preferred_element_type=F32)
                ycol = term if ycol is None else ycol + term
            cols.append(ycol)
        y = y + jnp.concatenate(cols, axis=1) + xs * dexp_ref[:, g * gw:(g + 1) * gw]

        xd = (xdt * eout_e).astype(BF16)
        s_ref[g] = state * ein_e[TBLK - 1:TBLK, :] + lax.dot_general(
            bb, xd, (((0,), (0,)), ((), ())), preferred_element_type=F32)

        yz = y * _silu(z_ref[0, :, g * gw:(g + 1) * gw])
        ms = jnp.mean(yz * yz, axis=-1, keepdims=True)
        y_ref[0, :, g * gw:(g + 1) * gw] = (
            yz * lax.rsqrt(ms + LN_EPS) * nw_ref[:, g * gw:(g + 1) * gw]).astype(y_ref.dtype)


def _ssd(p, ps, conv_w, conv_b, dt_bias, a_log, d_skip, norm_w, *, xbc_col, z_col):
    bsz, seq_len, _ = p.shape
    nc = pl.cdiv(seq_len, TBLK)
    bcw = SSD_CONV_CH - SSD_INNER
    xoff, boff, zoff = xbc_col // SSD_INNER, (xbc_col + SSD_INNER) // bcw, z_col // SSD_INNER
    expand = ((jnp.arange(SSD_INNER)[None, :] // SSD_HEADDIM) == jnp.arange(LANES)[:, None]).astype(BF16)
    pad = LANES - SSD_HEADS
    dtb = jnp.pad(dt_bias, (0, pad)).reshape(1, LANES)
    alog = jnp.pad(a_log, (0, pad)).reshape(1, LANES)
    dexp = jnp.repeat(d_skip, SSD_HEADDIM).reshape(1, SSD_INNER)
    cb2 = conv_b.reshape(1, SSD_CONV_CH)
    const = lambda shape, col=0: pl.BlockSpec(shape, lambda b, c: (0, col))
    return pl.pallas_call(
        functools.partial(_ssd_kernel, seq_len=seq_len),
        grid=(bsz, nc),
        in_specs=[pl.BlockSpec((1, TBLK, SSD_INNER), lambda b, c: (b, c, xoff)),
                  pl.BlockSpec((1, TBLK, bcw), lambda b, c: (b, c, boff)),
                  pl.BlockSpec((1, TBLK, SSD_INNER), lambda b, c: (b, c, zoff)),
                  pl.BlockSpec((1, TBLK, LANES), lambda b, c: (b, c, 0)),
                  const((LANES, SSD_INNER)),
                  const((SSD_CONV, SSD_INNER)), const((SSD_CONV, bcw), SSD_INNER // bcw),
                  const((1, SSD_INNER)), const((1, bcw), SSD_INNER // bcw),
                  const((1, LANES)), const((1, LANES)),
                  const((1, SSD_INNER)), const((1, SSD_INNER))],
        out_specs=pl.BlockSpec((1, TBLK, SSD_INNER), lambda b, c: (b, c, 0)),
        out_shape=jax.ShapeDtypeStruct((bsz, seq_len, SSD_INNER), BF16),
        scratch_shapes=[pltpu.VMEM((SSD_GROUPS, SSD_STATE, SSD_GROUP_W), F32),
                        pltpu.VMEM((8 + TBLK, SSD_INNER), F32),
                        pltpu.VMEM((8 + TBLK, bcw), F32),
                        pltpu.VMEM((TBLK, SSD_INNER), F32),
                        pltpu.VMEM((TBLK, bcw), F32)],
        compiler_params=_cparams(("parallel", "arbitrary"), 48),
        name="ssd_scan",
    )(p, p, p, ps, expand, conv_w, conv_w, cb2, cb2, dtb, alog, dexp, norm_w.reshape(1, SSD_INNER))


FOX_QTILE = 512
FOX_HPS = 2


def _fox_cum_kernel(p_ref, b_ref, o_ref, *, seq_len):
    nfull, tail = seq_len // TBLK, seq_len % TBLK
    r = lax.broadcasted_iota(jnp.int32, (TBLK, TBLK), 0)
    c = lax.broadcasted_iota(jnp.int32, (TBLK, TBLK), 1)
    upper = (r <= c).astype(F32)
    carry = jnp.zeros((LANES, 1), F32)
    for blk in range(nfull + (1 if tail else 0)):
        rows = TBLK if blk < nfull else tail
        lf = _log_sigmoid(p_ref[0, blk * TBLK:blk * TBLK + rows, :] + b_ref[...])
        if rows < TBLK:
            lf = jnp.concatenate([lf, jnp.zeros((TBLK - rows, LANES), F32)], axis=0)
        cs = jnp.dot(lf.T, upper, precision=HIGHEST, preferred_element_type=F32) + carry
        o_ref[0, :, blk * TBLK:(blk + 1) * TBLK] = cs
        carry = cs[:, TBLK - 1:TBLK]


def _fox_cum(p, fgate_b, *, f_lane):
    bsz, seq_len, _ = p.shape
    lp = pl.cdiv(seq_len, TBLK) * TBLK
    bias = jnp.pad(fgate_b, (f_lane, LANES - FOX_HEADS - f_lane)).reshape(1, LANES)
    return pl.pallas_call(
        functools.partial(_fox_cum_kernel, seq_len=seq_len),
        grid=(bsz,),
        in_specs=[pl.BlockSpec((1, seq_len, LANES), lambda b: (b, 0, 0)),
                  pl.BlockSpec((1, LANES), lambda b: (0, 0))],
        out_specs=pl.BlockSpec((1, LANES, lp), lambda b: (b, 0, 0)),
        out_shape=jax.ShapeDtypeStruct((bsz, LANES, lp), F32),
        compiler_params=_cparams(("parallel",), 32),
        name="fox_cum",
    )(p, bias)


def _fox_tiles(seq_len):
    aligned = (seq_len // TBLK) * TBLK
    tiles, start = [], 0
    while start < aligned:
        rows = min(FOX_QTILE, aligned - start)
        tiles.append((start, rows, start + rows))
        start += rows
    if seq_len > aligned:
        tiles.append((aligned, seq_len - aligned, aligned + TBLK))
    return tiles


def _fox_kernel(q_ref, k_ref, v_ref, ck_ref, o_ref, kt_ref, vb_ref, *, seq_len):
    for hd in range(FOX_HPS):
        hs = slice(hd * FOX_HEADDIM, (hd + 1) * FOX_HEADDIM)
        for lo in range(0, seq_len, TBLK):
            rows = min(TBLK, seq_len - lo)
            kb = k_ref[0, lo:lo + rows, hs]
            vb = v_ref[0, lo:lo + rows, hs]
            if rows < TBLK:
                zpad = jnp.zeros((TBLK - rows, FOX_HEADDIM), F32)
                kb = jnp.concatenate([kb, zpad], axis=0)
                vb = jnp.concatenate([vb, zpad], axis=0)
            kt_ref[hd, :, lo:lo + TBLK] = kb.T.astype(BF16)
            vb_ref[hd, lo:lo + TBLK, :] = vb.astype(BF16)

        for start, rows, lk in _fox_tiles(seq_len):
            q = (q_ref[0, start:start + rows, hs] * (FOX_HEADDIM ** -0.5)).astype(BF16)
            dw = lk - start
            r = lax.broadcasted_iota(jnp.int32, (rows, dw), 0)
            c = lax.broadcasted_iota(jnp.int32, (rows, dw), 1)
            sd = jnp.dot(q, kt_ref[hd, :, start:lk], preferred_element_type=F32) - ck_ref[0, hd, :, start:lk]
            sd = jnp.where(c <= r, sd, NEG_INF)
            m = jnp.max(sd, axis=-1, keepdims=True)
            if start:
                so = jnp.dot(q, kt_ref[hd, :, 0:start], preferred_element_type=F32) - ck_ref[0, hd, :, 0:start]
                m = jnp.maximum(m, jnp.max(so, axis=-1, keepdims=True))
            pd = jnp.exp(sd - m)
            l = jnp.sum(pd, axis=-1, keepdims=True)
            o = jnp.dot(pd.astype(BF16), vb_ref[hd, start:lk, :], preferred_element_type=F32)
            if start:
                po = jnp.exp(so - m)
                l = l + jnp.sum(po, axis=-1, keepdims=True)
                o = o + jnp.dot(po.astype(BF16), vb_ref[hd, 0:start, :], preferred_element_type=F32)
            o_ref[0, start:start + rows, hs] = (o / l).astype(o_ref.dtype)


def _fox(p, cum_t, *, q_col, f_lane):
    assert f_lane % FOX_HPS == 0
    f0 = f_lane // FOX_HPS
    bsz, seq_len, _ = p.shape
    lp = cum_t.shape[2]
    hw = FOX_HPS * FOX_HEADDIM
    groups = FOX_HEADS // FOX_HPS
    qo = q_col // hw
    ko = qo + groups
    vo = ko + groups
    ck = cum_t.reshape(bsz, LANES, 1, lp)
    return pl.pallas_call(
        functools.partial(_fox_kernel, seq_len=seq_len),
        grid=(bsz, groups),
        in_specs=[pl.BlockSpec((1, seq_len, hw), lambda b, h: (b, 0, qo + h)),
                  pl.BlockSpec((1, seq_len, hw), lambda b, h: (b, 0, ko + h)),
                  pl.BlockSpec((1, seq_len, hw), lambda b, h: (b, 0, vo + h)),
                  pl.BlockSpec((1, FOX_HPS, 1, lp), lambda b, h: (b, f0 + h, 0, 0))],
        out_specs=pl.BlockSpec((1, seq_len, hw), lambda b, h: (b, 0, h)),
        out_shape=jax.ShapeDtypeStruct((bsz, seq_len, FOX_WIDTH), BF16),
        scratch_shapes=[pltpu.VMEM((FOX_HPS, FOX_HEADDIM, lp), BF16),
                        pltpu.VMEM((FOX_HPS, lp, FOX_HEADDIM), BF16)],
        compiler_params=_cparams(("parallel", "parallel"), 48),
        name="fox_attn",
    )(p, p, p, ck)


GLA_TBLK = 256
GLA_HPS = 4


def _gla_kernel(q_ref, k_ref, v_ref, r_ref, glr_ref, wgu_ref, gb_ref, nw_ref, o_ref, st_ref, *, seq_len):
    t = pl.program_id(2)

    @pl.when(t == 0)
    def _():
        st_ref[...] = jnp.zeros_like(st_ref)

    valid = seq_len - t * GLA_TBLK
    rmask = lax.broadcasted_iota(jnp.int32, (GLA_TBLK, 1), 0) < valid
    glr = glr_ref[0].astype(BF16)
    tri = _tril(GLA_CHUNK)
    mid = GLA_CHUNK // 2
    for hd in range(GLA_HPS):
        ks = slice(hd * GLA_DK, (hd + 1) * GLA_DK)
        vs = slice(hd * GLA_DV, (hd + 1) * GLA_DV)
        gpre = jnp.dot(glr, wgu_ref[:, ks], preferred_element_type=F32) + gb_ref[:, ks]
        g_all = jnp.where(rmask, _log_sigmoid(gpre) * (1.0 / GLA_TAU), 0.0)
        q_all = jnp.where(rmask, q_ref[0, :, ks], 0.0) * (GLA_DK ** -0.5)
        k_all = jnp.where(rmask, k_ref[0, :, ks], 0.0)
        v_all = jnp.where(rmask, v_ref[0, :, vs], 0.0)
        outs = []
        for ci in range(GLA_TBLK // GLA_CHUNK):
            sl = slice(ci * GLA_CHUNK, (ci + 1) * GLA_CHUNK)
            qc, kc, vc, gc = q_all[sl], k_all[sl], v_all[sl], g_all[sl]
            vcb = vc.astype(BF16)
            bcum = _cumsum_rows(gc)
            bref = bcum[mid:mid + 1, :]
            btot = bcum[GLA_CHUNK - 1:GLA_CHUNK, :]
            qe = (qc * jnp.exp(bcum - bref)).astype(BF16)
            ke = (kc * jnp.exp(bref - bcum)).astype(BF16)
            scores = lax.dot_general(qe, ke, (((1,), (1,)), ((), ())), preferred_element_type=F32)
            scores = jnp.where(tri, scores, 0.0).astype(BF16)
            state_t = st_ref[hd]
            qd = (qc * jnp.exp(bcum)).astype(BF16)
            o = jnp.dot(scores, vcb, preferred_element_type=F32) + lax.dot_general(
                qd, state_t.astype(BF16), (((1,), (1,)), ((), ())), preferred_element_type=F32)
            kd = (kc * jnp.exp(btot - bcum)).astype(BF16)
            st_ref[hd] = state_t * jnp.exp(btot) + lax.dot_general(
                vcb, kd, (((0,), (0,)), ((), ())), preferred_element_type=F32)
            outs.append(o)
        o = jnp.concatenate(outs, axis=0)
        ms = jnp.mean(o * o, axis=-1, keepdims=True)
        o = o * lax.rsqrt(ms + LN_EPS) * nw_ref[...]
        o_ref[0, :, vs] = (o * _silu(r_ref[0, :, vs])).astype(o_ref.dtype)


def _gla(p, pr, pg, w_gate_up, gate_b, norm_w, *, q_col, k_col, v_col, r_col):
    bsz, seq_len, _ = p.shape
    nt = pl.cdiv(seq_len, GLA_TBLK)
    kw, vw = GLA_HPS * GLA_DK, GLA_HPS * GLA_DV
    qo, ko, vo, ro = q_col // kw, k_col // kw, v_col // vw, r_col // vw
    wgu = jnp.pad(w_gate_up, ((0, LANES - GLA_RANK), (0, 0))).astype(BF16)
    return pl.pallas_call(
        functools.partial(_gla_kernel, seq_len=seq_len),
        grid=(bsz, GLA_HEADS // GLA_HPS, nt),
        in_specs=[pl.BlockSpec((1, GLA_TBLK, kw), lambda b, h, t: (b, t, qo + h)),
                  pl.BlockSpec((1, GLA_TBLK, kw), lambda b, h, t: (b, t, ko + h)),
                  pl.BlockSpec((1, GLA_TBLK, vw), lambda b, h, t: (b, t, vo + h)),
                  pl.BlockSpec((1, GLA_TBLK, vw), lambda b, h, t: (b, t, ro + h)),
                  pl.BlockSpec((1, GLA_TBLK, LANES), lambda b, h, t: (b, t, 0)),
                  pl.BlockSpec((LANES, kw), lambda b, h, t: (0, h)),
                  pl.BlockSpec((1, kw), lambda b, h, t: (0, h)),
                  pl.BlockSpec((1, GLA_DV), lambda b, h, t: (0, 0))],
        out_specs=pl.BlockSpec((1, GLA_TBLK, vw), lambda b, h, t: (b, t, h)),
        out_shape=jax.ShapeDtypeStruct((bsz, seq_len, GLA_VWIDTH), BF16),
        scratch_shapes=[pltpu.VMEM((GLA_HPS, GLA_DV, GLA_DK), F32)],
        compiler_params=_cparams(("parallel", "parallel", "arbitrary"), 40),
        name="gla_scan",
    )(p, p, p, pr, pg, wgu, gate_b.reshape(1, GLA_KWIDTH), norm_w.reshape(1, GLA_DV))


CONV_CBLK = 256


def _conf_kernel(ua_ref, ug_ref, w_ref, b_ref, g_ref, be_ref, o_ref, ext_ref, acc_ref, sh_ref, *, seq_len):
    t = pl.program_id(1)

    @pl.when(t == 0)
    def _():
        ext_ref[0:CONV_HALO, :] = jnp.zeros((CONV_HALO, CONV_CH), F32)

    valid = seq_len - t * TBLK
    rmask = lax.broadcasted_iota(jnp.int32, (TBLK, 1), 0) < valid
    glu = jnp.where(rmask, ua_ref[0] * _sigmoid(ug_ref[0]), 0.0)
    ext_ref[CONV_HALO:CONV_HALO + TBLK, :] = glu
    base = CONV_HALO - (CONV_WIDTH - 1)
    for cb in range(CONV_CH // CONV_CBLK):
        cs = slice(cb * CONV_CBLK, (cb + 1) * CONV_CBLK)
        acc = jnp.broadcast_to(b_ref[:, cs], (TBLK, CONV_CBLK))
        for r in range(8):
            taps = [k for k in range(CONV_WIDTH) if (base + k) % 8 == r]
            span = 8 * max((base + k) // 8 for k in taps) + TBLK
            sh_ref[0:span, :] = ext_ref[r:r + span, cs]
            for k in taps:
                lo = 8 * ((base + k) // 8)
                acc = acc + w_ref[k:k + 1, cs] * sh_ref[lo:lo + TBLK, :]
        acc_ref[:, cs] = acc
    ext_ref[0:CONV_HALO, :] = ext_ref[TBLK:TBLK + CONV_HALO, :]
    u = acc_ref[...]
    uc = u - jnp.mean(u, axis=-1, keepdims=True)
    var = jnp.mean(uc * uc, axis=-1, keepdims=True)
    o_ref[0] = _silu(uc * lax.rsqrt(var + LN_EPS) * g_ref[...] + be_ref[...]).astype(o_ref.dtype)


def _conformer_conv(p, dw_w, dw_b, ln_g, ln_b, *, glu_col):
    bsz, seq_len, _ = p.shape
    nt = pl.cdiv(seq_len, TBLK)
    ao = glu_col // CONV_CH
    vec = pl.BlockSpec((1, CONV_CH), lambda b, t: (0, 0))
    return pl.pallas_call(
        functools.partial(_conf_kernel, seq_len=seq_len),
        grid=(bsz, nt),
        in_specs=[pl.BlockSpec((1, TBLK, CONV_CH), lambda b, t: (b, t, ao)),
                  pl.BlockSpec((1, TBLK, CONV_CH), lambda b, t: (b, t, ao + 1)),
                  pl.BlockSpec((CONV_WIDTH, CONV_CH), lambda b, t: (0, 0)),
                  vec, vec, vec],
        out_specs=pl.BlockSpec((1, TBLK, CONV_CH), lambda b, t: (b, t, 0)),
        out_shape=jax.ShapeDtypeStruct((bsz, seq_len, CONV_CH), BF16),
        scratch_shapes=[pltpu.VMEM((CONV_HALO + TBLK, CONV_CH), F32),
                        pltpu.VMEM((TBLK, CONV_CH), F32),
                        pltpu.VMEM((CONV_HALO + TBLK, CONV_CBLK), F32)],
        compiler_params=_cparams(("parallel", "arbitrary"), 32),
        name="conformer_conv",
    )(p, p, dw_w, dw_b.reshape(1, CONV_CH), ln_g.reshape(1, CONV_CH), ln_b.reshape(1, CONV_CH))


MM_BM = 1376
MM_BN = 512

EV_Z_COL = 0
EV_XBC_COL = SSD_INNER
EV_DT_COL = SSD_INNER + SSD_CONV_CH
EV_A_COLS = EV_DT_COL
EV_TAIL = EV_DT_COL + SSD_HEADS
EV_B_COLS = 3 * FOX_WIDTH
EV_F_LANE = SSD_HEADS
OD_Q_COL = 0
OD_K_COL = GLA_KWIDTH
OD_V_COL = 2 * GLA_KWIDTH
OD_GLR_COL = 2 * GLA_KWIDTH + GLA_VWIDTH
OD_A_COLS = OD_GLR_COL
OD_TAIL = OD_GLR_COL + GLA_RANK
OD_R_COL = 0
OD_GLU_COL = GLA_VWIDTH
OD_B_COLS = GLA_VWIDTH + 2 * CONV_CH


def _narrow_rows(*pieces):
    rows = jnp.concatenate(pieces, axis=0)
    return jnp.pad(rows, ((0, LANES - rows.shape[0]), (0, 0)))[None]


def _normed(y, g, b, name):
    hb, stats = _ln(y, g, b, name=name)
    return hb, (y, stats, g.reshape(1, -1), b.reshape(1, -1))


def _out_proj(a, c, w_out, layer, res, alpha, name):
    m = res[0].shape[0]
    return _matmul2(a.reshape(m, -1), c.reshape(m, -1), w_out, layer, res, alpha=alpha, bm=MM_BM, bn=256,
                    vmem_mib=61, name=name)


def _ffn(res, hb, w_gate, w_up, w_down_b, layer, alpha, name):
    mid = _swiglu_up(hb, w_gate, w_up, layer, bm=MM_BM, bn=256, vmem_mib=48, name=name + "_up")
    return _matmul(mid, w_down_b, layer, n_out=D_MODEL, bm=MM_BM // 2, bn=256, out_dtype=F32, vmem_mib=56,
                   name=name + "_down", res=res, alpha=alpha)


def kernel(x, meta_tokens, ev_w_in, ev_conv_w, ev_conv_b, ev_dt_bias, ev_a_log, ev_d_skip, ev_ssm_norm_w, ev_fgate_b, ev_w_out, od_w_in, od_w_gate_up, od_gate_b, od_gla_norm_w, od_dwconv_w, od_dwconv_b, od_conv_ln_g, od_conv_ln_b, od_w_out, ln_mix_g, ln_mix_b, ffn_w_gate, ffn_w_up, ffn_w_down, ln_ffn_g, ln_ffn_b):
    alpha = (2.0 * DEPTH) ** 0.25
    bsz = x.shape[0]
    meta = jnp.broadcast_to(meta_tokens[None].astype(x.dtype), (bsz, N_META, D_MODEL))
    h3 = jnp.concatenate([meta, x], axis=1)
    seq_len = h3.shape[1]
    h = h3.reshape(bsz * seq_len, D_MODEL)
    hb = h.astype(BF16)
    unit_stats = jnp.zeros((h.shape[0], LANES), F32).at[:, 1].set(1.0)
    res = (h, unit_stats, jnp.ones((1, D_MODEL), F32), jnp.zeros((1, D_MODEL), F32))
    w_down_b = ffn_w_down.astype(BF16)
    for i in range(DEPTH):
        j = i // 2
        if i % 2 == 0:
            w_nk = jnp.swapaxes(ev_w_in, 1, 2)
            w_qkv = w_nk[j, EV_TAIL:EV_TAIL + EV_B_COLS].astype(BF16)[None]
            w_narrow = _narrow_rows(w_nk[j, EV_DT_COL:EV_TAIL], w_nk[j, EV_TAIL + EV_B_COLS:])
            pa = _matmul(hb, w_nk, j, n_out=EV_A_COLS, bm=MM_BM, bn=MM_BN, out_dtype=F32, vmem_mib=52,
                         name="ev_in_a", w_is_nk=True).reshape(bsz, seq_len, EV_A_COLS)
            pb = _matmul(hb, w_qkv, 0, n_out=EV_B_COLS, bm=MM_BM, bn=MM_BN, out_dtype=F32, vmem_mib=52,
                         name="ev_in_b", w_is_nk=True).reshape(bsz, seq_len, EV_B_COLS)
            ps = _matmul(hb, w_narrow, 0, n_out=LANES, bm=MM_BM, bn=LANES, out_dtype=F32, vmem_mib=52,
                         name="ev_in_s", w_is_nk=True).reshape(bsz, seq_len, LANES)
            y = _ssd(pa, ps, ev_conv_w[j], ev_conv_b[j], ev_dt_bias[j], ev_a_log[j], ev_d_skip[j],
                     ev_ssm_norm_w[j], xbc_col=EV_XBC_COL, z_col=EV_Z_COL)
            cum_t = _fox_cum(ps, ev_fgate_b[j], f_lane=EV_F_LANE)
            o = _fox(pb, cum_t, q_col=0, f_lane=EV_F_LANE)
            y = _out_proj(y, o, ev_w_out, j, res, alpha, "ev_out")
        else:
            w_nk = jnp.swapaxes(od_w_in, 1, 2)
            w_tail = w_nk[j, OD_TAIL:].astype(BF16)[None]
            w_narrow = _narrow_rows(w_nk[j, OD_GLR_COL:OD_TAIL])
            pa = _matmul(hb, w_nk, j, n_out=OD_A_COLS, bm=MM_BM, bn=MM_BN, out_dtype=F32, vmem_mib=52,
                         name="od_in_a", w_is_nk=True).reshape(bsz, seq_len, OD_A_COLS)
            pb = _matmul(hb, w_tail, 0, n_out=OD_B_COLS, bm=MM_BM, bn=MM_BN, out_dtype=F32, vmem_mib=52,
                         name="od_in_b", w_is_nk=True).reshape(bsz, seq_len, OD_B_COLS)
            pg = _matmul(hb, w_narrow, 0, n_out=LANES, bm=MM_BM, bn=LANES, out_dtype=F32, vmem_mib=52,
                         name="od_in_s", w_is_nk=True).reshape(bsz, seq_len, LANES)
            o = _gla(pa, pb, pg, od_w_gate_up[j], od_gate_b[j], od_gla_norm_w[j],
                     q_col=OD_Q_COL, k_col=OD_K_COL, v_col=OD_V_COL, r_col=OD_R_COL)
            u = _conformer_conv(pb, od_dwconv_w[j], od_dwconv_b[j], od_conv_ln_g[j], od_conv_ln_b[j],
                                glu_col=OD_GLU_COL)
            y = _out_proj(o, u, od_w_out, j, res, alpha, "od_out")
        hb, res = _normed(y, ln_mix_g[i], ln_mix_b[i], "mix_ln")
        y = _ffn(res, hb, ffn_w_gate, ffn_w_up, w_down_b, i, alpha, "ffn")
        if i + 1 < DEPTH:
            hb, res = _normed(y, ln_ffn_g[i], ln_ffn_b[i], "ffn_ln")
    return _ln_drop(y.reshape(bsz, seq_len, D_MODEL), ln_ffn_g[DEPTH - 1], ln_ffn_b[DEPTH - 1], drop=N_META,
                    name="final_ln")
```

```python
import functools

import jax
import jax.numpy as jnp
from jax import lax
from jax.experimental import pallas as pl
from jax.experimental.pallas import tpu as pltpu

F32 = jnp.float32
BF16 = jnp.bfloat16
HIGHEST = lax.Precision.HIGHEST

D_MODEL = 4096
DEPTH = 2
N_META = 16
SSD_HEADDIM = 64
SSD_INNER = D_MODEL
SSD_HEADS = SSD_INNER // SSD_HEADDIM
SSD_GROUPS = 8
SSD_HPG = SSD_HEADS // SSD_GROUPS
SSD_STATE = 128
SSD_CONV = 4
SSD_GROUP_W = SSD_HPG * SSD_HEADDIM
SSD_CONV_CH = SSD_INNER + 2 * SSD_GROUPS * SSD_STATE
FOX_HEADDIM = 128
FOX_WIDTH = D_MODEL // 2
FOX_HEADS = FOX_WIDTH // FOX_HEADDIM
NEG_INF = -1e30
GLA_HEADS = D_MODEL // 512
GLA_DK = 256
GLA_DV = 512
GLA_KWIDTH = GLA_HEADS * GLA_DK
GLA_VWIDTH = GLA_HEADS * GLA_DV
GLA_RANK = 16
GLA_TAU = 16.0
GLA_CHUNK = 64
CONV_CH = D_MODEL // 2
CONV_WIDTH = 31
LN_EPS = 1e-5

LANES = 128
TBLK = 128
CONV_HALO = 32
MIB = 1024 * 1024


def _cparams(semantics, vmem_mib):
    return pltpu.CompilerParams(dimension_semantics=semantics, vmem_limit_bytes=vmem_mib * MIB)


def _tile(dim, target):
    return dim if dim < target else target


def _sigmoid(x):
    return 1.0 / (1.0 + jnp.exp(-x))


def _silu(x):
    return x * _sigmoid(x)


def _softplus(x):
    return jnp.maximum(x, 0.0) + jnp.log1p(jnp.exp(-jnp.abs(x)))


def _log_sigmoid(x):
    return jnp.minimum(x, 0.0) - jnp.log1p(jnp.exp(-jnp.abs(x)))


def _cumsum_rows(x):
    n = x.shape[0]
    row = lax.broadcasted_iota(jnp.int32, (n, 1), 0)
    shift = 1
    while shift < n:
        x = x + jnp.where(row >= shift, pltpu.roll(x, shift, axis=0), 0.0)
        shift *= 2
    return x


def _tril(n):
    r = lax.broadcasted_iota(jnp.int32, (n, n), 0)
    c = lax.broadcasted_iota(jnp.int32, (n, n), 1)
    return c <= r


def _mm_kernel(x_ref, w_ref, o_ref):
    o_ref[...] = jnp.dot(x_ref[...], w_ref[...].astype(BF16), preferred_element_type=F32).astype(o_ref.dtype)


def _mm_nt_kernel(x_ref, wt_ref, o_ref):
    o_ref[...] = lax.dot_general(x_ref[...], wt_ref[...].astype(BF16), (((1,), (1,)), ((), ())),
                                 preferred_element_type=F32).astype(o_ref.dtype)


def _residual(y_ref, st_ref, g_ref, b_ref):
    st = st_ref[...]
    return (y_ref[...] - st[:, 0:1]) * st[:, 1:2] * g_ref[...] + b_ref[...]


def _residual_specs(bm, bn):
    return [pl.BlockSpec((bm, bn), lambda i, j: (i, j)), pl.BlockSpec((bm, LANES), lambda i, j: (i, 0)),
            pl.BlockSpec((1, bn), lambda i, j: (0, j)), pl.BlockSpec((1, bn), lambda i, j: (0, j))]


def _mm_res_kernel(x_ref, w_ref, y_ref, st_ref, g_ref, b_ref, o_ref, *, alpha):
    acc = jnp.dot(x_ref[...], w_ref[...].astype(BF16), preferred_element_type=F32)
    o_ref[...] = alpha * _residual(y_ref, st_ref, g_ref, b_ref) + acc


def _matmul(x, w, layer, *, n_out, bm, bn, out_dtype, vmem_mib, name, res=None, alpha=None, w_is_nk=False):
    m, k = x.shape
    bm = _tile(m, bm)
    tile = pl.BlockSpec((bm, bn), lambda i, j: (i, j))
    if w_is_nk:
        body, wspec = _mm_nt_kernel, pl.BlockSpec((None, bn, k), lambda i, j: (layer, j, 0))
    else:
        body, wspec = _mm_kernel, pl.BlockSpec((None, k, bn), lambda i, j: (layer, 0, j))
    in_specs = [pl.BlockSpec((bm, k), lambda i, j: (i, 0)), wspec]
    args = (x, w)
    if res is not None:
        assert not w_is_nk
        body, args = functools.partial(_mm_res_kernel, alpha=alpha), (x, w, *res)
        in_specs += _residual_specs(bm, bn)
    return pl.pallas_call(
        body,
        grid=(pl.cdiv(m, bm), n_out // bn),
        in_specs=in_specs,
        out_specs=tile,
        out_shape=jax.ShapeDtypeStruct((m, n_out), out_dtype),
        compiler_params=_cparams(("parallel", "arbitrary"), vmem_mib),
        name=name,
    )(*args)


def _mm2_kernel(a_ref, c_ref, w1_ref, w2_ref, y_ref, st_ref, g_ref, b_ref, o_ref, *, alpha):
    acc = jnp.dot(a_ref[...], w1_ref[...].astype(BF16), preferred_element_type=F32)
    acc = acc + jnp.dot(c_ref[...], w2_ref[...].astype(BF16), preferred_element_type=F32)
    o_ref[...] = alpha * _residual(y_ref, st_ref, g_ref, b_ref) + acc


def _matmul2(a, c, w, layer, res, *, alpha, bm, bn, vmem_mib, name):
    m, k1 = a.shape
    k2 = c.shape[1]
    n = w.shape[2]
    bm = _tile(m, bm)
    return pl.pallas_call(
        functools.partial(_mm2_kernel, alpha=alpha),
        grid=(pl.cdiv(m, bm), n // bn),
        in_specs=[pl.BlockSpec((bm, k1), lambda i, j: (i, 0)),
                  pl.BlockSpec((bm, k2), lambda i, j: (i, 0)),
                  pl.BlockSpec((None, k1, bn), lambda i, j: (layer, 0, j)),
                  pl.BlockSpec((None, k2, bn), lambda i, j: (layer, k1 // k2, j))] + _residual_specs(bm, bn),
        out_specs=pl.BlockSpec((bm, bn), lambda i, j: (i, j)),
        out_shape=jax.ShapeDtypeStruct((m, n), F32),
        compiler_params=_cparams(("parallel", "arbitrary"), vmem_mib),
        name=name,
    )(a, c, w, w, *res)


def _swiglu_up_kernel(x_ref, wg_ref, wu_ref, o_ref):
    x = x_ref[...]
    g = jnp.dot(x, wg_ref[...].astype(BF16), preferred_element_type=F32)
    u = jnp.dot(x, wu_ref[...].astype(BF16), preferred_element_type=F32)
    o_ref[...] = (_silu(g) * u).astype(o_ref.dtype)


def _swiglu_up(x, wg, wu, layer, *, bm, bn, vmem_mib, name):
    m, k = x.shape
    n = wg.shape[2]
    bm = _tile(m, bm)
    wspec = pl.BlockSpec((None, k, bn), lambda i, j: (layer, 0, j))
    return pl.pallas_call(
        _swiglu_up_kernel,
        grid=(pl.cdiv(m, bm), n // bn),
        in_specs=[pl.BlockSpec((bm, k), lambda i, j: (i, 0)), wspec, wspec],
        out_specs=pl.BlockSpec((bm, bn), lambda i, j: (i, j)),
        out_shape=jax.ShapeDtypeStruct((m, n), BF16),
        compiler_params=_cparams(("parallel", "arbitrary"), vmem_mib),
        name=name,
    )(x, wg, wu)


LN_ROWS = 512


def _row_stats(y):
    mean = jnp.mean(y, axis=-1, keepdims=True)
    yc = y - mean
    var = jnp.mean(yc * yc, axis=-1, keepdims=True)
    return mean, lax.rsqrt(var + LN_EPS)


def _ln_kernel(y_ref, g_ref, b_ref, ob_ref, st_ref):
    y = y_ref[...]
    mean, rstd = _row_stats(y)
    ob_ref[...] = ((y - mean) * rstd * g_ref[...] + b_ref[...]).astype(BF16)
    lane = lax.broadcasted_iota(jnp.int32, st_ref.shape, 1)
    st_ref[...] = jnp.where(lane == 0, mean, jnp.where(lane == 1, rstd, 0.0))


def _ln(y, g, b, *, name):
    m, d = y.shape
    bm = _tile(m, LN_ROWS)
    row = pl.BlockSpec((bm, d), lambda i: (i, 0))
    vec = pl.BlockSpec((1, d), lambda i: (0, 0))
    return pl.pallas_call(
        _ln_kernel,
        grid=(pl.cdiv(m, bm),),
        in_specs=[row, vec, vec],
        out_specs=[row, pl.BlockSpec((bm, LANES), lambda i: (i, 0))],
        out_shape=[jax.ShapeDtypeStruct((m, d), BF16), jax.ShapeDtypeStruct((m, LANES), F32)],
        compiler_params=_cparams(("parallel",), 48),
        name=name,
    )(y, g.reshape(1, d), b.reshape(1, d))


def _ln_drop_kernel(ya_ref, yb_ref, g_ref, b_ref, o_ref, *, drop):
    y = jnp.concatenate([ya_ref[0, drop:, :], yb_ref[0]], axis=0)
    mean, rstd = _row_stats(y)
    o_ref[0] = (y - mean) * rstd * g_ref[...] + b_ref[...]


def _ln_drop(y, g, b, *, drop, name):
    bsz, seq_len, d = y.shape
    rows = LN_ROWS // 2
    nt = (seq_len - drop) // rows
    assert (seq_len - drop) % rows == 0 and rows % drop == 0 and drop % 8 == 0
    vec = pl.BlockSpec((1, d), lambda bi, t: (0, 0))
    return pl.pallas_call(
        functools.partial(_ln_drop_kernel, drop=drop),
        grid=(bsz, nt),
        in_specs=[pl.BlockSpec((1, rows, d), lambda bi, t: (bi, t, 0)),
                  pl.BlockSpec((1, drop, d), lambda bi, t: (bi, (t + 1) * (rows // drop), 0)),
                  vec, vec],
        out_specs=pl.BlockSpec((1, rows, d), lambda bi, t: (bi, t, 0)),
        out_shape=jax.ShapeDtypeStruct((bsz, seq_len - drop, d), F32),
        compiler_params=_cparams(("parallel", "parallel"), 32),
        name=name,
    )(y, y, g.reshape(1, d), b.reshape(1, d))


def _ssd_kernel(xs_ref, bc_ref, z_ref, dt_ref, e_ref, cwx_ref, cwb_ref, cbx_ref, cbb_ref,
                dtb_ref, alog_ref, dexp_ref, nw_ref, y_ref,
                s_ref, extx_ref, extb_ref, actx_ref, actb_ref, *, seq_len):
    c = pl.program_id(1)

    @pl.when(c == 0)
    def _():
        s_ref[...] = jnp.zeros_like(s_ref)
        extx_ref[0:8, :] = jnp.zeros((8, SSD_INNER), F32)
        extb_ref[0:8, :] = jnp.zeros((8, SSD_CONV_CH - SSD_INNER), F32)

    valid = seq_len - c * TBLK
    rmask = lax.broadcasted_iota(jnp.int32, (TBLK, 1), 0) < valid

    def conv_silu(raw_ref, ext_ref, w_ref, bias_ref, act_ref):
        raw = jnp.where(rmask, raw_ref[0], 0.0)
        ext_ref[8:8 + TBLK, :] = raw
        acc = bias_ref[...] + w_ref[SSD_CONV - 1:SSD_CONV, :] * raw
        for k in range(SSD_CONV - 1):
            lo = 8 - (SSD_CONV - 1) + k
            acc = acc + w_ref[k:k + 1, :] * ext_ref[lo:lo + TBLK, :]
        ext_ref[0:8, :] = raw[TBLK - 8:TBLK, :]
        act_ref[...] = _silu(acc)

    conv_silu(xs_ref, extx_ref, cwx_ref, cbx_ref, actx_ref)
    conv_silu(bc_ref, extb_ref, cwb_ref, cbb_ref, actb_ref)

    dt_all = jnp.where(rmask, _softplus(dt_ref[0] + dtb_ref[...]), 0.0)
    la_all = dt_all * (-jnp.exp(alog_ref[...]))
    tri = _tril(TBLK)
    a_cs = jnp.dot(tri.astype(F32), la_all, precision=HIGHEST, preferred_element_type=F32)
    a_cs_t = a_cs.T
    e_in = jnp.exp(a_cs)
    e_out = jnp.exp(a_cs[TBLK - 1:TBLK, :] - a_cs)
    stack = jnp.concatenate([dt_all, e_in, e_out], axis=0)
    stack_hi = stack.astype(BF16)
    stack_lo = (stack - stack_hi.astype(F32)).astype(BF16)
    lane = lax.broadcasted_iota(jnp.int32, (TBLK, LANES), 1)
    gw, st = SSD_GROUP_W, SSD_STATE
    for g in range(SSD_GROUPS):
        eg = e_ref[:, g * gw:(g + 1) * gw]
        expanded = (jnp.dot(stack_hi, eg, preferred_element_type=F32)
                    + jnp.dot(stack_lo, eg, preferred_element_type=F32))
        dt_e = expanded[0:TBLK]
        ein_e = expanded[TBLK:2 * TBLK]
        eout_e = expanded[2 * TBLK:3 * TBLK]
        xs = actx_ref[:, g * gw:(g + 1) * gw]
        bb = actb_ref[:, g * st:(g + 1) * st].astype(BF16)
        cc = actb_ref[:, (SSD_GROUPS + g) * st:(SSD_GROUPS + g + 1) * st].astype(BF16)
        xdt = xs * dt_e
        state = s_ref[g]
        y = jnp.dot(cc, state.astype(BF16), preferred_element_type=F32) * ein_e
        cb = lax.dot_general(cc, bb, (((1,), (1,)), ((), ())), preferred_element_type=F32)
        cols = []
        for m in range(gw // LANES):
            xcol = xdt[:, m * LANES:(m + 1) * LANES]
            ycol = None
            for half in range(LANES // SSD_HEADDIM):
                hh = g * SSD_HPG + m * (LANES // SSD_HEADDIM) + half
                colb = jnp.broadcast_to(a_cs[:, hh:hh + 1], (TBLK, TBLK))
                rowb = jnp.broadcast_to(a_cs_t[hh:hh + 1, :], (TBLK, TBLK))
                lmat = jnp.where(tri, jnp.exp(colb - rowb), 0.0) * cb
                in_half = (lane >= half * SSD_HEADDIM) & (lane < (half + 1) * SSD_HEADDIM)
                xh = jnp.where(in_half, xcol, 0.0).astype(BF16)
                term = jnp.dot(lmat.astype(BF16), xh, preferred_element_type=F32)
                ycol = term if ycol is None else ycol + term
            cols.append(ycol)
        y = y + jnp.concatenate(cols, axis=1) + xs * dexp_ref[:, g * gw:(g + 1) * gw]

        xd = (xdt * eout_e).astype(BF16)
        s_ref[g] = state * ein_e[TBLK - 1:TBLK, :] + lax.dot_general(
            bb, xd, (((0,), (0,)), ((), ())), preferred_element_type=F32)

        yz = y * _silu(z_ref[0, :, g * gw:(g + 1) * gw])
        ms = jnp.mean(yz * yz, axis=-1, keepdims=True)
        y_ref[0, :, g * gw:(g + 1) * gw] = (
            yz * lax.rsqrt(ms + LN_EPS) * nw_ref[:, g * gw:(g + 1) * gw]).astype(y_ref.dtype)


def _ssd(p, ps, conv_w, conv_b, dt_bias, a_log, d_skip, norm_w, *, xbc_col, z_col):
    bsz, seq_len, _ = p.shape
    nc = pl.cdiv(seq_len, TBLK)
    bcw = SSD_CONV_CH - SSD_INNER
    xoff, boff, zoff = xbc_col // SSD_INNER, (xbc_col + SSD_INNER) // bcw, z_col // SSD_INNER
    expand = ((jnp.arange(SSD_INNER)[None, :] // SSD_HEADDIM) == jnp.arange(LANES)[:, None]).astype(BF16)
    pad = LANES - SSD_HEADS
    dtb = jnp.pad(dt_bias, (0, pad)).reshape(1, LANES)
    alog = jnp.pad(a_log, (0, pad)).reshape(1, LANES)
    dexp = jnp.repeat(d_skip, SSD_HEADDIM).reshape(1, SSD_INNER)
    cb2 = conv_b.reshape(1, SSD_CONV_CH)
    const = lambda shape, col=0: pl.BlockSpec(shape, lambda b, c: (0, col))
    return pl.pallas_call(
        functools.partial(_ssd_kernel, seq_len=seq_len),
        grid=(bsz, nc),
        in_specs=[pl.BlockSpec((1, TBLK, SSD_INNER), lambda b, c: (b, c, xoff)),
                  pl.BlockSpec((1, TBLK, bcw), lambda b, c: (b, c, boff)),
                  pl.BlockSpec((1, TBLK, SSD_INNER), lambda b, c: (b, c, zoff)),
                  pl.BlockSpec((1, TBLK, LANES), lambda b, c: (b, c, 0)),
                  const((LANES, SSD_INNER)),
                  const((SSD_CONV, SSD_INNER)), const((SSD_CONV, bcw), SSD_INNER // bcw),
                  const((1, SSD_INNER)), const((1, bcw), SSD_INNER // bcw),
                  const((1, LANES)), const((1, LANES)),
                  const((1, SSD_INNER)), const((1, SSD_INNER))],
        out_specs=pl.BlockSpec((1, TBLK, SSD_INNER), lambda b, c: (b, c, 0)),
        out_shape=jax.ShapeDtypeStruct((bsz, seq_len, SSD_INNER), BF16),
        scratch_shapes=[pltpu.VMEM((SSD_GROUPS, SSD_STATE, SSD_GROUP_W), F32),
                        pltpu.VMEM((8 + TBLK, SSD_INNER), F32),
                        pltpu.VMEM((8 + TBLK, bcw), F32),
                        pltpu.VMEM((TBLK, SSD_INNER), F32),
                        pltpu.VMEM((TBLK, bcw), F32)],
        compiler_params=_cparams(("parallel", "arbitrary"), 48),
        name="ssd_scan",
    )(p, p, p, ps, expand, conv_w, conv_w, cb2, cb2, dtb, alog, dexp, norm_w.reshape(1, SSD_INNER))


FOX_QTILE = 512
FOX_HPS = 2


def _fox_cum_kernel(p_ref, b_ref, o_ref, *, seq_len):
    nfull, tail = seq_len // TBLK, seq_len % TBLK
    r = lax.broadcasted_iota(jnp.int32, (TBLK, TBLK), 0)
    c = lax.broadcasted_iota(jnp.int32, (TBLK, TBLK), 1)
    upper = (r <= c).astype(F32)
    carry = jnp.zeros((LANES, 1), F32)
    for blk in range(nfull + (1 if tail else 0)):
        rows = TBLK if blk < nfull else tail
        lf = _log_sigmoid(p_ref[0, blk * TBLK:blk * TBLK + rows, :] + b_ref[...])
        if rows < TBLK:
            lf = jnp.concatenate([lf, jnp.zeros((TBLK - rows, LANES), F32)], axis=0)
        cs = jnp.dot(lf.T, upper, precision=HIGHEST, preferred_element_type=F32) + carry
        o_ref[0, :, blk * TBLK:(blk + 1) * TBLK] = cs
        carry = cs[:, TBLK - 1:TBLK]


def _fox_cum(p, fgate_b, *, f_lane):
    bsz, seq_len, _ = p.shape
    lp = pl.cdiv(seq_len, TBLK) * TBLK
    bias = jnp.pad(fgate_b, (f_lane, LANES - FOX_HEADS - f_lane)).reshape(1, LANES)
    return pl.pallas_call(
        functools.partial(_fox_cum_kernel, seq_len=seq_len),
        grid=(bsz,),
        in_specs=[pl.BlockSpec((1, seq_len, LANES), lambda b: (b, 0, 0)),
                  pl.BlockSpec((1, LANES), lambda b: (0, 0))],
        out_specs=pl.BlockSpec((1, LANES, lp), lambda b: (b, 0, 0)),
        out_shape=jax.ShapeDtypeStruct((bsz, LANES, lp), F32),
        compiler_params=_cparams(("parallel",), 32),
        name="fox_cum",
    )(p, bias)


def _fox_tiles(seq_len):
    aligned = (seq_len // TBLK) * TBLK
    tiles, start = [], 0
    while start < aligned:
        rows = min(FOX_QTILE, aligned - start)
        tiles.append((start, rows, start + rows))
        start += rows
    if seq_len > aligned:
        tiles.append((aligned, seq_len - aligned, aligned + TBLK))
    return tiles


def _fox_kernel(q_ref, k_ref, v_ref, ck_ref, o_ref, kt_ref, vb_ref, *, seq_len):
    for hd in range(FOX_HPS):
        hs = slice(hd * FOX_HEADDIM, (hd + 1) * FOX_HEADDIM)
        for lo in range(0, seq_len, TBLK):
            rows = min(TBLK, seq_len - lo)
            kb = k_ref[0, lo:lo + rows, hs]
            vb = v_ref[0, lo:lo + rows, hs]
            if rows < TBLK:
                zpad = jnp.zeros((TBLK - rows, FOX_HEADDIM), F32)
                kb = jnp.concatenate([kb, zpad], axis=0)
                vb = jnp.concatenate([vb, zpad], axis=0)
            kt_ref[hd, :, lo:lo + TBLK] = kb.T.astype(BF16)
            vb_ref[hd, lo:lo + TBLK, :] = vb.astype(BF16)

        for start, rows, lk in _fox_tiles(seq_len):
            q = (q_ref[0, start:start + rows, hs] * (FOX_HEADDIM ** -0.5)).astype(BF16)
            dw = lk - start
            r = lax.broadcasted_iota(jnp.int32, (rows, dw), 0)
            c = lax.broadcasted_iota(jnp.int32, (rows, dw), 1)
            sd = jnp.dot(q, kt_ref[hd, :, start:lk], preferred_element_type=F32) - ck_ref[0, hd, :, start:lk]
            sd = jnp.where(c <= r, sd, NEG_INF)
            m = jnp.max(sd, axis=-1, keepdims=True)
            if start:
                so = jnp.dot(q, kt_ref[hd, :, 0:start], preferred_element_type=F32) - ck_ref[0, hd, :, 0:start]
                m = jnp.maximum(m, jnp.max(so, axis=-1, keepdims=True))
            pd = jnp.exp(sd - m)
            l = jnp.sum(pd, axis=-1, keepdims=True)
            o = jnp.dot(pd.astype(BF16), vb_ref[hd, start:lk, :], preferred_element_type=F32)
            if start:
                po = jnp.exp(so - m)
                l = l + jnp.sum(po, axis=-1, keepdims=True)
                o = o + jnp.dot(po.astype(BF16), vb_ref[hd, 0:start, :], preferred_element_type=F32)
            o_ref[0, start:start + rows, hs] = (o / l).astype(o_ref.dtype)


def _fox(p, cum_t, *, q_col, f_lane):
    assert f_lane % FOX_HPS == 0
    f0 = f_lane // FOX_HPS
    bsz, seq_len, _ = p.shape
    lp = cum_t.shape[2]
    hw = FOX_HPS * FOX_HEADDIM
    groups = FOX_HEADS // FOX_HPS
    qo = q_col // hw
    ko = qo + groups
    vo = ko + groups
    ck = cum_t.reshape(bsz, LANES, 1, lp)
    return pl.pallas_call(
        functools.partial(_fox_kernel, seq_len=seq_len),
        grid=(bsz, groups),
        in_specs=[pl.BlockSpec((1, seq_len, hw), lambda b, h: (b, 0, qo + h)),
                  pl.BlockSpec((1, seq_len, hw), lambda b, h: (b, 0, ko + h)),
                  pl.BlockSpec((1, seq_len, hw), lambda b, h: (b, 0, vo + h)),
                  pl.BlockSpec((1, FOX_HPS, 1, lp), lambda b, h: (b, f0 + h, 0, 0))],
        out_specs=pl.BlockSpec((1, seq_len, hw), lambda b, h: (b, 0, h)),
        out_shape=jax.ShapeDtypeStruct((bsz, seq_len, FOX_WIDTH), BF16),
        scratch_shapes=[pltpu.VMEM((FOX_HPS, FOX_HEADDIM, lp), BF16),
                        pltpu.VMEM((FOX_HPS, lp, FOX_HEADDIM), BF16)],
        compiler_params=_cparams(("parallel", "parallel"), 48),
        name="fox_attn",
    )(p, p, p, ck)


GLA_TBLK = 256
GLA_HPS = 4


def _gla_kernel(q_ref, k_ref, v_ref, r_ref, glr_ref, wgu_ref, gb_ref, nw_ref, o_ref, st_ref, *, seq_len):
    t = pl.program_id(2)

    @pl.when(t == 0)
    def _():
        st_ref[...] = jnp.zeros_like(st_ref)

    valid = seq_len - t * GLA_TBLK
    rmask = lax.broadcasted_iota(jnp.int32, (GLA_TBLK, 1), 0) < valid
    glr = glr_ref[0].astype(BF16)
    tri = _tril(GLA_CHUNK)
    mid = GLA_CHUNK // 2
    for hd in range(GLA_HPS):
        ks = slice(hd * GLA_DK, (hd + 1) * GLA_DK)
        vs = slice(hd * GLA_DV, (hd + 1) * GLA_DV)
        gpre = jnp.dot(glr, wgu_ref[:, ks], preferred_element_type=F32) + gb_ref[:, ks]
        g_all = jnp.where(rmask, _log_sigmoid(gpre) * (1.0 / GLA_TAU), 0.0)
        q_all = jnp.where(rmask, q_ref[0, :, ks], 0.0) * (GLA_DK ** -0.5)
        k_all = jnp.where(rmask, k_ref[0, :, ks], 0.0)
        v_all = jnp.where(rmask, v_ref[0, :, vs], 0.0)
        outs = []
        for ci in range(GLA_TBLK // GLA_CHUNK):
            sl = slice(ci * GLA_CHUNK, (ci + 1) * GLA_CHUNK)
            qc, kc, vc, gc = q_all[sl], k_all[sl], v_all[sl], g_all[sl]
            vcb = vc.astype(BF16)
            bcum = _cumsum_rows(gc)
            bref = bcum[mid:mid + 1, :]
            btot = bcum[GLA_CHUNK - 1:GLA_CHUNK, :]
            qe = (qc * jnp.exp(bcum - bref)).astype(BF16)
            ke = (kc * jnp.exp(bref - bcum)).astype(BF16)
            scores = lax.dot_general(qe, ke, (((1,), (1,)), ((), ())), preferred_element_type=F32)
            scores = jnp.where(tri, scores, 0.0).astype(BF16)
            state_t = st_ref[hd]
            qd = (qc * jnp.exp(bcum)).astype(BF16)
            o = jnp.dot(scores, vcb, preferred_element_type=F32) + lax.dot_general(
                qd, state_t.astype(BF16), (((1,), (1,)), ((), ())), preferred_element_type=F32)
            kd = (kc * jnp.exp(btot - bcum)).astype(BF16)
            st_ref[hd] = state_t * jnp.exp(btot) + lax.dot_general(
                vcb, kd, (((0,), (0,)), ((), ())), preferred_element_type=F32)
            outs.append(o)
        o = jnp.concatenate(outs, axis=0)
        ms = jnp.mean(o * o, axis=-1, keepdims=True)
        o = o * lax.rsqrt(ms + LN_EPS) * nw_ref[...]
        o_ref[0, :, vs] = (o * _silu(r_ref[0, :, vs])).astype(o_ref.dtype)


def _gla(p, pr, pg, w_gate_up, gate_b, norm_w, *, q_col, k_col, v_col, r_col):
    bsz, seq_len, _ = p.shape
    nt = pl.cdiv(seq_len, GLA_TBLK)
    kw, vw = GLA_HPS * GLA_DK, GLA_HPS * GLA_DV
    qo, ko, vo, ro = q_col // kw, k_col // kw, v_col // vw, r_col // vw
    wgu = jnp.pad(w_gate_up, ((0, LANES - GLA_RANK), (0, 0))).astype(BF16)
    return pl.pallas_call(
        functools.partial(_gla_kernel, seq_len=seq_len),
        grid=(bsz, GLA_HEADS // GLA_HPS, nt),
        in_specs=[pl.BlockSpec((1, GLA_TBLK, kw), lambda b, h, t: (b, t, qo + h)),
                  pl.BlockSpec((1, GLA_TBLK, kw), lambda b, h, t: (b, t, ko + h)),
                  pl.BlockSpec((1, GLA_TBLK, vw), lambda b, h, t: (b, t, vo + h)),
                  pl.BlockSpec((1, GLA_TBLK, vw), lambda b, h, t: (b, t, ro + h)),
                  pl.BlockSpec((1, GLA_TBLK, LANES), lambda b, h, t: (b, t, 0)),
                  pl.BlockSpec((LANES, kw), lambda b, h, t: (0, h)),
                  pl.BlockSpec((1, kw), lambda b, h, t: (0, h)),
                  pl.BlockSpec((1, GLA_DV), lambda b, h, t: (0, 0))],
        out_specs=pl.BlockSpec((1, GLA_TBLK, vw), lambda b, h, t: (b, t, h)),
        out_shape=jax.ShapeDtypeStruct((bsz, seq_len, GLA_VWIDTH), BF16),
        scratch_shapes=[pltpu.VMEM((GLA_HPS, GLA_DV, GLA_DK), F32)],
        compiler_params=_cparams(("parallel", "parallel", "arbitrary"), 40),
        name="gla_scan",
    )(p, p, p, pr, pg, wgu, gate_b.reshape(1, GLA_KWIDTH), norm_w.reshape(1, GLA_DV))


CONV_CBLK = 256


def _conf_kernel(ua_ref, ug_ref, w_ref, b_ref, g_ref, be_ref, o_ref, ext_ref, acc_ref, sh_ref, *, seq_len):
    t = pl.program_id(1)

    @pl.when(t == 0)
    def _():
        ext_ref[0:CONV_HALO, :] = jnp.zeros((CONV_HALO, CONV_CH), F32)

    valid = seq_len - t * TBLK
    rmask = lax.broadcasted_iota(jnp.int32, (TBLK, 1), 0) < valid
    glu = jnp.where(rmask, ua_ref[0] * _sigmoid(ug_ref[0]), 0.0)
    ext_ref[CONV_HALO:CONV_HALO + TBLK, :] = glu
    base = CONV_HALO - (CONV_WIDTH - 1)
    for cb in range(CONV_CH // CONV_CBLK):
        cs = slice(cb * CONV_CBLK, (cb + 1) * CONV_CBLK)
        acc = jnp.broadcast_to(b_ref[:, cs], (TBLK, CONV_CBLK))
        for r in range(8):
            taps = [k for k in range(CONV_WIDTH) if (base + k) % 8 == r]
            span = 8 * max((base + k) // 8 for k in taps) + TBLK
            sh_ref[0:span, :] = ext_ref[r:r + span, cs]
            for k in taps:
                lo = 8 * ((base + k) // 8)
                acc = acc + w_ref[k:k + 1, cs] * sh_ref[lo:lo + TBLK, :]
        acc_ref[:, cs] = acc
    ext_ref[0:CONV_HALO, :] = ext_ref[TBLK:TBLK + CONV_HALO, :]
    u = acc_ref[...]
    uc = u - jnp.mean(u, axis=-1, keepdims=True)
    var = jnp.mean(uc * uc, axis=-1, keepdims=True)
    o_ref[0] = _silu(uc * lax.rsqrt(var + LN_EPS) * g_ref[...] + be_ref[...]).astype(o_ref.dtype)


def _conformer_conv(p, dw_w, dw_b, ln_g, ln_b, *, glu_col):
    bsz, seq_len, _ = p.shape
    nt = pl.cdiv(seq_len, TBLK)
    ao = glu_col // CONV_CH
    vec = pl.BlockSpec((1, CONV_CH), lambda b, t: (0, 0))
    return pl.pallas_call(
        functools.partial(_conf_kernel, seq_len=seq_len),
        grid=(bsz, nt),
        in_specs=[pl.BlockSpec((1, TBLK, CONV_CH), lambda b, t: (b, t, ao)),
                  pl.BlockSpec((1, TBLK, CONV_CH), lambda b, t: (b, t, ao + 1)),
                  pl.BlockSpec((CONV_WIDTH, CONV_CH), lambda b, t: (0, 0)),
                  vec, vec, vec],
        out_specs=pl.BlockSpec((1, TBLK, CONV_CH), lambda b, t: (b, t, 0)),
        out_shape=jax.ShapeDtypeStruct((bsz, seq_len, CONV_CH), BF16),
        scratch_shapes=[pltpu.VMEM((CONV_HALO + TBLK, CONV_CH), F32),
                        pltpu.VMEM((TBLK, CONV_CH), F32),
                        pltpu.VMEM((CONV_HALO + TBLK, CONV_CBLK), F32)],
        compiler_params=_cparams(("parallel", "arbitrary"), 32),
        name="conformer_conv",
    )(p, p, dw_w, dw_b.reshape(1, CONV_CH), ln_g.reshape(1, CONV_CH), ln_b.reshape(1, CONV_CH))


MM_BM = 1376
MM_BN = 512

EV_Z_COL = 0
EV_XBC_COL = SSD_INNER
EV_DT_COL = SSD_INNER + SSD_CONV_CH
EV_A_COLS = EV_DT_COL
EV_TAIL = EV_DT_COL + SSD_HEADS
EV_B_COLS = 3 * FOX_WIDTH
EV_F_LANE = SSD_HEADS
OD_Q_COL = 0
OD_K_COL = GLA_KWIDTH
OD_V_COL = 2 * GLA_KWIDTH
OD_GLR_COL = 2 * GLA_KWIDTH + GLA_VWIDTH
OD_A_COLS = OD_GLR_COL
OD_TAIL = OD_GLR_COL + GLA_RANK
OD_R_COL = 0
OD_GLU_COL = GLA_VWIDTH
OD_B_COLS = GLA_VWIDTH + 2 * CONV_CH


def _narrow_rows(*pieces):
    rows = jnp.concatenate(pieces, axis=0)
    return jnp.pad(rows, ((0, LANES - rows.shape[0]), (0, 0)))[None]


def _normed(y, g, b, name):
    hb, stats = _ln(y, g, b, name=name)
    return hb, (y, stats, g.reshape(1, -1), b.reshape(1, -1))


def _out_proj(a, c, w_out, layer, res, alpha, name):
    m = res[0].shape[0]
    return _matmul2(a.reshape(m, -1), c.reshape(m, -1), w_out, layer, res, alpha=alpha, bm=MM_BM, bn=256,
                    vmem_mib=61, name=name)


def _ffn(res, hb, w_gate, w_up, w_down_b, layer, alpha, name):
    mid = _swiglu_up(hb, w_gate, w_up, layer, bm=MM_BM, bn=256, vmem_mib=48, name=name + "_up")
    return _matmul(mid, w_down_b, layer, n_out=D_MODEL, bm=MM_BM // 2, bn=256, out_dtype=F32, vmem_mib=56,
                   name=name + "_down", res=res, alpha=alpha)


def kernel(x, meta_tokens, ev_w_in, ev_conv_w, ev_conv_b, ev_dt_bias, ev_a_log, ev_d_skip, ev_ssm_norm_w, ev_fgate_b, ev_w_out, od_w_in, od_w_gate_up, od_gate_b, od_gla_norm_w, od_dwconv_w, od_dwconv_b, od_conv_ln_g, od_conv_ln_b, od_w_out, ln_mix_g, ln_mix_b, ffn_w_gate, ffn_w_up, ffn_w_down, ln_ffn_g, ln_ffn_b):
    alpha = (2.0 * DEPTH) ** 0.25
    bsz = x.shape[0]
    meta = jnp.broadcast_to(meta_tokens[None].astype(x.dtype), (bsz, N_META, D_MODEL))
    h3 = jnp.concatenate([meta, x], axis=1)
    seq_len = h3.shape[1]
    h = h3.reshape(bsz * seq_len, D_MODEL)
    hb = h.astype(BF16)
    unit_stats = jnp.zeros((h.shape[0], LANES), F32).at[:, 1].set(1.0)
    res = (h, unit_stats, jnp.ones((1, D_MODEL), F32), jnp.zeros((1, D_MODEL), F32))
    w_down_b = ffn_w_down.astype(BF16)
    for i in range(DEPTH):
        j = i // 2
        if i % 2 == 0:
            w_nk = jnp.swapaxes(ev_w_in, 1, 2)
            w_qkv = w_nk[j, EV_TAIL:EV_TAIL + EV_B_COLS].astype(BF16)[None]
            w_narrow = _narrow_rows(w_nk[j, EV_DT_COL:EV_TAIL], w_nk[j, EV_TAIL + EV_B_COLS:])
            pa = _matmul(hb, w_nk, j, n_out=EV_A_COLS, bm=MM_BM, bn=MM_BN, out_dtype=F32, vmem_mib=52,
                         name="ev_in_a", w_is_nk=True).reshape(bsz, seq_len, EV_A_COLS)
            pb = _matmul(hb, w_qkv, 0, n_out=EV_B_COLS, bm=MM_BM, bn=MM_BN, out_dtype=F32, vmem_mib=52,
                         name="ev_in_b", w_is_nk=True).reshape(bsz, seq_len, EV_B_COLS)
            ps = _matmul(hb, w_narrow, 0, n_out=LANES, bm=MM_BM, bn=LANES, out_dtype=F32, vmem_mib=52,
                         name="ev_in_s", w_is_nk=True).reshape(bsz, seq_len, LANES)
            y = _ssd(pa, ps, ev_conv_w[j], ev_conv_b[j], ev_dt_bias[j], ev_a_log[j], ev_d_skip[j],
                     ev_ssm_norm_w[j], xbc_col=EV_XBC_COL, z_col=EV_Z_COL)
            cum_t = _fox_cum(ps, ev_fgate_b[j], f_lane=EV_F_LANE)
            o = _fox(pb, cum_t, q_col=0, f_lane=EV_F_LANE)
            y = _out_proj(y, o, ev_w_out, j, res, alpha, "ev_out")
        else:
            w_nk = jnp.swapaxes(od_w_in, 1, 2)
            w_tail = w_nk[j, OD_TAIL:].astype(BF16)[None]
            w_narrow = _narrow_rows(w_nk[j, OD_GLR_COL:OD_TAIL])
            pa = _matmul(hb, w_nk, j, n_out=OD_A_COLS, bm=MM_BM, bn=MM_BN, out_dtype=F32, vmem_mib=52,
                         name="od_in_a", w_is_nk=True).reshape(bsz, seq_len, OD_A_COLS)
            pb = _matmul(hb, w_tail, 0, n_out=OD_B_COLS, bm=MM_BM, bn=MM_BN, out_dtype=F32, vmem_mib=52,
                         name="od_in_b", w_is_nk=True).reshape(bsz, seq_len, OD_B_COLS)
            pg = _matmul(hb, w_narrow, 0, n_out=LANES, bm=MM_BM, bn=LANES, out_dtype=F32, vmem_mib=52,
                         name="od_in_s", w_is_nk=True).reshape(bsz, seq_len, LANES)
            o = _gla(pa, pb, pg, od_w_gate_up[j], od_gate_b[j], od_gla_norm_w[j],
                     q_col=OD_Q_COL, k_col=OD_K_COL, v_col=OD_V_COL, r_col=OD_R_COL)
            u = _conformer_conv(pb, od_dwconv_w[j], od_dwconv_b[j], od_conv_ln_g[j], od_conv_ln_b[j],
                                glu_col=OD_GLU_COL)
            y = _out_proj(o, u, od_w_out, j, res, alpha, "od_out")
        hb, res = _normed(y, ln_mix_g[i], ln_mix_b[i], "mix_ln")
        y = _ffn(res, hb, ffn_w_gate, ffn_w_up, w_down_b, i, alpha, "ffn")
        if i + 1 < DEPTH:
            hb, res = _normed(y, ln_ffn_g[i], ln_ffn_b[i], "ffn_ln")
    return _ln_drop(y.reshape(bsz, seq_len, D_MODEL), ln_ffn_g[DEPTH - 1], ln_ffn_b[DEPTH - 1], drop=N_META,
                    name="final_ln")
```
